```python
import math
import jax, jax.numpy as jnp
from jax import lax
import numpy as np

D_MODEL = 1024
BATCH = 8
SEQ = 2048
DEPTH = 1
DEC_BATCH = 4
DEC_SEQ = 4096
PAST_LEN = 128

GRID_W = 64
N_HEADS = 16
N_KV_HEADS = 4
HEAD_DIM = 64
ATTN_Q_WIDTH = N_HEADS * HEAD_DIM
ATTN_KV_WIDTH = N_KV_HEADS * HEAD_DIM
ROPE_THETA = 10000.0
Q_BLOCK = 128
SG_WIDTH = 1024
SG_GROUPS = 8
SG_GROUP_DIM = SG_WIDTH // SG_GROUPS
SG_CHUNK = 128
N_BRANCHES = 2
IN_SPLITS = (ATTN_Q_WIDTH,
             ATTN_Q_WIDTH + ATTN_KV_WIDTH,
             ATTN_Q_WIDTH + 2 * ATTN_KV_WIDTH,
             ATTN_Q_WIDTH + 2 * ATTN_KV_WIDTH + 2 * SG_WIDTH)
IN_WIDTH = ATTN_Q_WIDTH + 2 * ATTN_KV_WIDTH + 2 * SG_WIDTH + N_BRANCHES * D_MODEL
N_MEM = 256
XA_HEADS = 4
XA_HEAD_DIM = D_MODEL // XA_HEADS
PEER_HEADS = 8
PEER_N_KEYS = 128
PEER_N_EXPERTS = PEER_N_KEYS * PEER_N_KEYS
PEER_D_KEY = 256
PEER_HALF = PEER_D_KEY // 2
PEER_TOPK = 16
PEER_TOKEN_BLOCK = 128
EPS = 1e-6

kernel_name = 'hybrid_gqa_sgu_peer_encoder'


def rmsnorm(x, g):
    x32 = x.astype(jnp.float32)
    y = x32 * lax.rsqrt(jnp.mean(x32 * x32, axis=-1, keepdims=True) + EPS)
    return (y * g.astype(jnp.float32)).astype(x.dtype)


def axial_rope_tables(n_tokens):
    rows = n_tokens // GRID_W
    row_pos = jnp.repeat(jnp.arange(rows, dtype=jnp.float32), GRID_W)
    col_pos = jnp.tile(jnp.arange(GRID_W, dtype=jnp.float32), rows)
    n_freq = HEAD_DIM // 4
    inv_freq = ROPE_THETA ** (-jnp.arange(n_freq, dtype=jnp.float32) / n_freq)
    ang_r = row_pos[:, None] * inv_freq[None, :]
    ang_c = col_pos[:, None] * inv_freq[None, :]
    return jnp.cos(ang_r), jnp.sin(ang_r), jnp.cos(ang_c), jnp.sin(ang_c)


def _rotate(x, cos, sin):
    x1, x2 = jnp.split(x, 2, axis=-1)
    c = cos[None, :, None, :]
    s = sin[None, :, None, :]
    return jnp.concatenate([x1 * c - x2 * s, x2 * c + x1 * s], axis=-1)


def apply_axial_rope(x, cos_r, sin_r, cos_c, sin_c):
    x32 = x.astype(jnp.float32)
    xr, xc = jnp.split(x32, 2, axis=-1)
    out = jnp.concatenate([_rotate(xr, cos_r, sin_r), _rotate(xc, cos_c, sin_c)], axis=-1)
    return out.astype(x.dtype)


def block_attention(q, k, v):
    B, S = q.shape[0], q.shape[1]
    n_blk = S // Q_BLOCK
    grp = N_HEADS // N_KV_HEADS
    scale = 1.0 / math.sqrt(HEAD_DIM)
    qb = q.reshape(B, n_blk, Q_BLOCK, N_KV_HEADS, grp, HEAD_DIM).transpose(1, 0, 3, 4, 2, 5)

    def one_block(q_blk):
        s = jnp.einsum('bkgqd,bskd->bkgqs', q_blk, k).astype(jnp.float32) * scale
        p = jax.nn.softmax(s, axis=-1).astype(v.dtype)
        return jnp.einsum('bkgqs,bskd->bkgqd', p, v)

    o = lax.map(one_block, qb)
    return o.transpose(1, 0, 4, 2, 3, 5).reshape(B, S, ATTN_Q_WIDTH)


def spatial_gating(z, norm_g, w_s, b_s):
    B, S = z.shape[0], z.shape[1]
    u, v = jnp.split(z, 2, axis=-1)
    v = rmsnorm(v, norm_g)
    vc = v.reshape(B, S // SG_CHUNK, SG_CHUNK, SG_GROUPS, SG_GROUP_DIM)
    sv = jnp.einsum('gpq,bnqgc->bnpgc', w_s, vc) + b_s.T[:, :, None]
    return u * sv.reshape(B, S, SG_WIDTH)


def cross_attention(h, mem_n, wq, wkv, wo):
    B, S = h.shape[0], h.shape[1]
    M = mem_n.shape[1]
    q = (h @ wq).reshape(B, S, XA_HEADS, XA_HEAD_DIM)
    k, v = jnp.split(mem_n @ wkv, 2, axis=-1)
    k = k.reshape(B, M, XA_HEADS, XA_HEAD_DIM)
    v = v.reshape(B, M, XA_HEADS, XA_HEAD_DIM)
    s = jnp.einsum('bshd,bmhd->bhsm', q, k).astype(jnp.float32) * (1.0 / math.sqrt(XA_HEAD_DIM))
    p = jax.nn.softmax(s, axis=-1).astype(v.dtype)
    o = jnp.einsum('bhsm,bmhd->bshd', p, v).reshape(B, S, D_MODEL)
    return o @ wo


def peer_ffn(h, w_q, sub_k1, sub_k2, e_down, e_up):
    B, S, D = h.shape
    n_tok = B * S
    hb = h.reshape(n_tok // PEER_TOKEN_BLOCK, PEER_TOKEN_BLOCK, D)

    def one_block(xb):
        n = xb.shape[0]
        q = (xb @ w_q).reshape(n, PEER_HEADS, PEER_D_KEY)
        q1, q2 = jnp.split(q, 2, axis=-1)
        s1 = jnp.einsum('nhd,hkd->nhk', q1, sub_k1).astype(jnp.float32)
        s2 = jnp.einsum('nhd,hkd->nhk', q2, sub_k2).astype(jnp.float32)
        v1, i1 = lax.top_k(s1, PEER_TOPK)
        v2, i2 = lax.top_k(s2, PEER_TOPK)
        cand = (v1[..., :, None] + v2[..., None, :]).reshape(n, PEER_HEADS, PEER_TOPK * PEER_TOPK)
        top_s, top_c = lax.top_k(cand, PEER_TOPK)
        e1 = jnp.take_along_axis(i1, top_c // PEER_TOPK, axis=-1)
        e2 = jnp.take_along_axis(i2, top_c % PEER_TOPK, axis=-1)
        idx = e1 * PEER_N_KEYS + e2
        g = jax.nn.softmax(top_s, axis=-1).astype(xb.dtype)
        act = jax.nn.gelu(jnp.einsum('nhkd,nd->nhk', e_down[idx], xb))
        return jnp.einsum('nhk,nhkd->nd', g * act, e_up[idx])

    return lax.map(one_block, hb).reshape(B, S, D)


def encoder_layer(x, mem, norm_mix_g, w_in, q_norm_g, k_norm_g, sg_norm_g, sg_w, sg_b,
                  w_attn_o, w_sg_o, w_out, norm_xa_g, norm_mem_g, wq_xa, wkv_xa, wo_xa,
                  norm_ffn_g, w_peer_q, peer_k1, peer_k2, expert_down, expert_up):
    B, S = x.shape[0], x.shape[1]
    h = rmsnorm(x, norm_mix_g)
    proj = h @ w_in
    q, k, v, z_sg, gate_logits = jnp.split(proj, IN_SPLITS, axis=-1)
    q = rmsnorm(q.reshape(B, S, N_HEADS, HEAD_DIM), q_norm_g)
    k = rmsnorm(k.reshape(B, S, N_KV_HEADS, HEAD_DIM), k_norm_g)
    v = v.reshape(B, S, N_KV_HEADS, HEAD_DIM)
    cos_r, sin_r, cos_c, sin_c = axial_rope_tables(S)
    q = apply_axial_rope(q, cos_r, sin_r, cos_c, sin_c)
    k = apply_axial_rope(k, cos_r, sin_r, cos_c, sin_c)
    attn_branch = block_attention(q, k, v) @ w_attn_o
    sg_branch = spatial_gating(jax.nn.gelu(z_sg), sg_norm_g, sg_w, sg_b) @ w_sg_o
    g_attn, g_sg = jnp.split(jax.nn.sigmoid(gate_logits), N_BRANCHES, axis=-1)
    x = x + (g_attn * attn_branch + g_sg * sg_branch) @ w_out
    x = x + cross_attention(rmsnorm(x, norm_xa_g), rmsnorm(mem, norm_mem_g), wq_xa, wkv_xa, wo_xa)
    x = x + peer_ffn(rmsnorm(x, norm_ffn_g), w_peer_q, peer_k1, peer_k2, expert_down, expert_up)
    return x


def setup_inputs(seed: int = 0) -> dict:
    key = jax.random.key(seed)
    ks = jax.random.split(key, 32)
    f32 = jnp.float32

    def nrm(k, shape, scale):
        return jax.random.normal(k, shape, dtype=f32) * scale

    def gain(k, shape):
        return 1.0 + 0.02 * jax.random.normal(k, shape, dtype=f32)

    return {
        'x_prompt': nrm(ks[0], (BATCH, SEQ, D_MODEL), 1.0),
        'x_sample': nrm(ks[1], (DEC_BATCH, DEC_SEQ, D_MODEL), 1.0),
        'mem_prompt': nrm(ks[2], (BATCH, N_MEM, D_MODEL), 1.0),
        'mem_sample': nrm(ks[3], (DEC_BATCH, N_MEM, D_MODEL), 1.0),
        'norm_mix_g': gain(ks[4], (DEPTH, D_MODEL)),
        'w_in': nrm(ks[5], (DEPTH, D_MODEL, IN_WIDTH), D_MODEL ** -0.5),
        'q_norm_g': gain(ks[6], (DEPTH, HEAD_DIM)),
        'k_norm_g': gain(ks[7], (DEPTH, HEAD_DIM)),
        'sg_norm_g': gain(ks[8], (DEPTH, SG_WIDTH)),
        'sg_w': nrm(ks[9], (DEPTH, SG_GROUPS, SG_CHUNK, SG_CHUNK), 0.5 * SG_CHUNK ** -0.5),
        'sg_b': 1.0 + nrm(ks[10], (DEPTH, SG_GROUPS, SG_CHUNK), 0.1),
        'w_attn_o': nrm(ks[11], (DEPTH, ATTN_Q_WIDTH, D_MODEL), ATTN_Q_WIDTH ** -0.5),
        'w_sg_o': nrm(ks[12], (DEPTH, SG_WIDTH, D_MODEL), SG_WIDTH ** -0.5),
        'w_out': nrm(ks[13], (DEPTH, D_MODEL, D_MODEL), D_MODEL ** -0.5),
        'norm_xa_g': gain(ks[14], (DEPTH, D_MODEL)),
        'norm_mem_g': gain(ks[15], (DEPTH, D_MODEL)),
        'wq_xa': nrm(ks[16], (DEPTH, D_MODEL, D_MODEL), D_MODEL ** -0.5),
        'wkv_xa': nrm(ks[17], (DEPTH, D_MODEL, 2 * D_MODEL), D_MODEL ** -0.5),
        'wo_xa': nrm(ks[18], (DEPTH, D_MODEL, D_MODEL), D_MODEL ** -0.5),
        'norm_ffn_g': gain(ks[19], (DEPTH, D_MODEL)),
        'w_peer_q': nrm(ks[20], (DEPTH, D_MODEL, PEER_HEADS * PEER_D_KEY), D_MODEL ** -0.5),
        'peer_k1': nrm(ks[21], (DEPTH, PEER_HEADS, PEER_N_KEYS, PEER_HALF), PEER_HALF ** -0.5),
        'peer_k2': nrm(ks[22], (DEPTH, PEER_HEADS, PEER_N_KEYS, PEER_HALF), PEER_HALF ** -0.5),
        'expert_down': nrm(ks[23], (DEPTH, PEER_N_EXPERTS, D_MODEL), D_MODEL ** -0.5),
        'expert_up': nrm(ks[24], (DEPTH, PEER_N_EXPERTS, D_MODEL), 0.5),
        'final_norm_g': gain(ks[25], (D_MODEL,)),
    }


def reference(x_prompt, x_sample, mem_prompt, mem_sample, norm_mix_g, w_in, q_norm_g, k_norm_g,
              sg_norm_g, sg_w, sg_b, w_attn_o, w_sg_o, w_out, norm_xa_g, norm_mem_g, wq_xa, wkv_xa,
              wo_xa, norm_ffn_g, w_peer_q, peer_k1, peer_k2, expert_down, expert_up, final_norm_g):
    xp = x_prompt
    xs = x_sample
    for l in range(DEPTH):
        params = (norm_mix_g[l], w_in[l], q_norm_g[l], k_norm_g[l], sg_norm_g[l], sg_w[l], sg_b[l],
                  w_attn_o[l], w_sg_o[l], w_out[l], norm_xa_g[l], norm_mem_g[l], wq_xa[l], wkv_xa[l],
                  wo_xa[l], norm_ffn_g[l], w_peer_q[l], peer_k1[l], peer_k2[l], expert_down[l],
                  expert_up[l])
        xp = encoder_layer(xp, mem_prompt, *params)
        xs = encoder_layer(xs, mem_sample, *params)
    y_prompt = rmsnorm(xp, final_norm_g)
    y_sample = rmsnorm(xs, final_norm_g)
    return (y_prompt, y_sample)
```

```python
import functools
import math

import jax
import jax.numpy as jnp
from jax import lax
from jax.experimental import pallas as pl
from jax.experimental.pallas import tpu as pltpu

D_MODEL = 1024
GRID_W = 64
N_HEADS = 16
N_KV_HEADS = 4
HEAD_DIM = 64
KV_GROUP = N_HEADS // N_KV_HEADS
ATTN_Q_WIDTH = N_HEADS * HEAD_DIM
ATTN_KV_WIDTH = N_KV_HEADS * HEAD_DIM
ROPE_THETA = 10000.0
SG_WIDTH = 1024
SG_GROUPS = 8
SG_GROUP_DIM = SG_WIDTH // SG_GROUPS
SG_CHUNK = 128
XA_HEADS = 4
XA_HEAD_DIM = D_MODEL // XA_HEADS
PEER_HEADS = 8
PEER_N_KEYS = 128
PEER_D_KEY = 256
PEER_HALF = PEER_D_KEY // 2
PEER_TOPK = 16
EPS = 1e-6

_Q0 = 0
_K0 = _Q0 + ATTN_Q_WIDTH
_V0 = _K0 + ATTN_KV_WIDTH
_Z0 = _V0 + ATTN_KV_WIDTH
_G0 = _Z0 + 2 * SG_WIDTH
_IN_WIDTH = _G0 + 2 * D_MODEL

LANES = 128
SUBLANES = 8
V7X_VMEM_BYTES = 64 * 1024 * 1024

MXU_DTYPE = jnp.bfloat16
NEG_INF = float("-inf")


def _dot(a, b):
    return jnp.dot(a, b, preferred_element_type=jnp.float32)


def _dot_nt(a, b):
    return lax.dot_general(a, b, (((1,), (1,)), ((), ())), preferred_element_type=jnp.float32)


def _gelu(x):
    c = math.sqrt(2.0 / math.pi)
    return 0.5 * x * (1.0 + jnp.tanh(c * (x + 0.044715 * (x * x * x))))


def _sigmoid(x):
    return 1.0 / (1.0 + jnp.exp(-x))


def _rmsnorm_rows(x, g_row):
    ms = jnp.mean(x * x, axis=-1, keepdims=True)
    return x * lax.rsqrt(ms + EPS) * g_row


def _rmsnorm_cols(xT, g_col):
    ms = jnp.mean(xT * xT, axis=0, keepdims=True)
    return xT * lax.rsqrt(ms + EPS) * g_col


def _params(semantics, vmem_mb):
    return pltpu.CompilerParams(dimension_semantics=semantics, vmem_limit_bytes=vmem_mb * 1024 * 1024)


def _const_spec(shape):
    nd = len(shape)
    return pl.BlockSpec(shape, lambda *_: (0,) * nd)


def _kv_mem_kernel(mem_ref, g_ref, wk_ref, wvT_ref, k_ref, vT_ref):
    mn = _rmsnorm_rows(mem_ref[0], g_ref[...]).astype(MXU_DTYPE)
    k_ref[0] = _dot(mn, wk_ref[...]).astype(k_ref.dtype)
    vT_ref[0] = _dot_nt(wvT_ref[...], mn).astype(vT_ref.dtype)


def _kv_mem(mem, g_row, wk, wvT):
    nb, m, d = mem.shape
    return pl.pallas_call(
        _kv_mem_kernel,
        out_shape=(jax.ShapeDtypeStruct((nb, m, d), MXU_DTYPE), jax.ShapeDtypeStruct((nb, d, m), MXU_DTYPE)),
        grid=(nb,),
        in_specs=[pl.BlockSpec((1, m, d), lambda b: (b, 0, 0)), _const_spec((1, d)),
                  _const_spec((d, d)), _const_spec((d, d))],
        out_specs=(pl.BlockSpec((1, m, d), lambda b: (b, 0, 0)), pl.BlockSpec((1, d, m), lambda b: (b, 0, 0))),
        compiler_params=_params(("parallel",), 32),
        name="kv_mem",
    )(mem, g_row, wk, wvT)


def _head_norm_rope(t, g_col, cos, sin):
    ms = jnp.mean(t * t, axis=0, keepdims=True)
    t = t * lax.rsqrt(ms + EPS) * g_col
    q4 = HEAD_DIM // 4
    sw = jnp.concatenate([t[q4:2 * q4], t[0:q4], t[3 * q4:], t[2 * q4:3 * q4]], axis=0)
    return t * cos + sw * sin


def _inproj_kernel(x_ref, gmix_ref, w_ref, gq_ref, gk_ref, cos_ref, sin_ref, gsg_ref, wsT_ref, bs_ref, wsgo_ref,
                   qT_ref, k_ref, vT_ref, gattn_ref, sgp_ref):
    tm = x_ref.shape[1]
    h = _rmsnorm_rows(x_ref[0], gmix_ref[...]).astype(MXU_DTYPE)

    def proj_t(lo, hi):
        return _dot_nt(w_ref[lo:hi, :], h)

    cos = cos_ref[...]
    sin = sin_ref[...]

    q_t = proj_t(_Q0, _K0)
    scale = 1.0 / math.sqrt(HEAD_DIM)
    for hd in range(N_HEADS):
        r = _head_norm_rope(q_t[hd * HEAD_DIM:(hd + 1) * HEAD_DIM], gq_ref[...], cos, sin) * scale
        qT_ref[0, hd * HEAD_DIM:(hd + 1) * HEAD_DIM, :] = r.astype(qT_ref.dtype)

    k_t = proj_t(_K0, _V0)
    k_rot = jnp.concatenate(
        [_head_norm_rope(k_t[g * HEAD_DIM:(g + 1) * HEAD_DIM], gk_ref[...], cos, sin) for g in range(N_KV_HEADS)],
        axis=0)
    k_tok = k_rot.T
    for g in range(N_KV_HEADS):
        k_ref[0, g] = k_tok[:, g * HEAD_DIM:(g + 1) * HEAD_DIM].astype(k_ref.dtype)

    vT_ref[0] = proj_t(_V0, _Z0).astype(vT_ref.dtype)

    z_t = _gelu(proj_t(_Z0, _G0))
    u = z_t[:SG_WIDTH]
    vn = _rmsnorm_cols(z_t[SG_WIDTH:], gsg_ref[...]).astype(MXU_DTYPE)
    n_chunks = tm // SG_CHUNK
    sv_groups = []
    for g in range(SG_GROUPS):
        vg = vn[g * SG_GROUP_DIM:(g + 1) * SG_GROUP_DIM]
        lhs = jnp.concatenate([vg[:, c * SG_CHUNK:(c + 1) * SG_CHUNK] for c in range(n_chunks)], axis=0)
        r = _dot(lhs, wsT_ref[g]) + bs_ref[g]
        sv_groups.append(jnp.concatenate(
            [r[c * SG_GROUP_DIM:(c + 1) * SG_GROUP_DIM] for c in range(n_chunks)], axis=1))
    sg = (u * jnp.concatenate(sv_groups, axis=0)).astype(MXU_DTYPE)
    sg_branch = _dot(wsgo_ref[...], sg)

    gates = _sigmoid(proj_t(_G0, _IN_WIDTH))
    gattn_ref[0] = gates[:D_MODEL].astype(gattn_ref.dtype)
    sgp_ref[0] = (gates[D_MODEL:] * sg_branch).astype(sgp_ref.dtype)


def _inproj(x, gmix, w_inT, gq, gk, cos_t, sin_t, gsg, wsT, bs, wsgoT, tm):
    b, s, d = x.shape
    grid = (b, s // tm)
    tok = lambda rows: pl.BlockSpec((1, rows, tm), lambda i, j: (i, 0, j))
    return pl.pallas_call(
        _inproj_kernel,
        out_shape=(jax.ShapeDtypeStruct((b, ATTN_Q_WIDTH, s), MXU_DTYPE),
                   jax.ShapeDtypeStruct((b, N_KV_HEADS, s, HEAD_DIM), MXU_DTYPE),
                   jax.ShapeDtypeStruct((b, ATTN_KV_WIDTH, s), MXU_DTYPE),
                   jax.ShapeDtypeStruct((b, D_MODEL, s), MXU_DTYPE),
                   jax.ShapeDtypeStruct((b, D_MODEL, s), jnp.float32)),
        grid=grid,
        in_specs=[pl.BlockSpec((1, tm, d), lambda i, j: (i, j, 0)),
                  _const_spec((1, d)), _const_spec(w_inT.shape),
                  _const_spec((HEAD_DIM, 1)), _const_spec((HEAD_DIM, 1)),
                  pl.BlockSpec((HEAD_DIM, tm), lambda i, j: (0, j)),
                  pl.BlockSpec((HEAD_DIM, tm), lambda i, j: (0, j)),
                  _const_spec((SG_WIDTH, 1)), _const_spec(wsT.shape), _const_spec(bs.shape),
                  _const_spec(wsgoT.shape)],
        out_specs=(tok(ATTN_Q_WIDTH),
                   pl.BlockSpec((1, N_KV_HEADS, tm, HEAD_DIM), lambda i, j: (i, 0, j, 0)),
                   tok(ATTN_KV_WIDTH), tok(D_MODEL), tok(D_MODEL)),
        compiler_params=_params(("parallel", "parallel"), 56),
        name="inproj",
    )(x, gmix, w_inT, gq, gk, cos_t, sin_t, gsg, wsT, bs, wsgoT)


def _attn_kernel(qT_ref, k_ref, vT_ref, o_ref):
    k = k_ref[0, 0]
    vT = vT_ref[0]
    for a in range(KV_GROUP):
        qT = qT_ref[0, a * HEAD_DIM:(a + 1) * HEAD_DIM, :]
        sT = _dot(k, qT)
        m = jnp.max(sT, axis=0, keepdims=True)
        p = jnp.exp(sT - m)
        l = jnp.sum(p, axis=0, keepdims=True)
        oT = _dot(vT, p.astype(MXU_DTYPE)) / l
        o_ref[0, a * HEAD_DIM:(a + 1) * HEAD_DIM, :] = oT.astype(o_ref.dtype)


def _attention(qT, k, vT, tq):
    b, _, s = qT.shape
    rows = KV_GROUP * HEAD_DIM
    return pl.pallas_call(
        _attn_kernel,
        out_shape=jax.ShapeDtypeStruct((b, ATTN_Q_WIDTH, s), MXU_DTYPE),
        grid=(b, N_KV_HEADS, s // tq),
        in_specs=[pl.BlockSpec((1, rows, tq), lambda i, g, j: (i, g, j)),
                  pl.BlockSpec((1, 1, s, HEAD_DIM), lambda i, g, j: (i, g, 0, 0)),
                  pl.BlockSpec((1, HEAD_DIM, s), lambda i, g, j: (i, g, 0))],
        out_specs=pl.BlockSpec((1, rows, tq), lambda i, g, j: (i, g, j)),
        compiler_params=_params(("parallel", "parallel", "parallel"), 48),
        name="attn",
    )(qT, k, vT)


def _merge_kernel(x_ref, attn_ref, gattn_ref, sgp_ref, wao_ref, wout_ref, gxa_ref, wq_ref, km_ref, vmT_ref,
                  wo_ref, gffn_ref, x2_ref, hp_ref):
    xT = x_ref[0].T
    attn_branch = _dot(wao_ref[...], attn_ref[0])
    mix = gattn_ref[0].astype(jnp.float32) * attn_branch + sgp_ref[0]
    x1 = xT + _dot(wout_ref[...], mix.astype(MXU_DTYPE))

    hx = _rmsnorm_cols(x1, gxa_ref[...]).astype(MXU_DTYPE)
    qx = (_dot(wq_ref[...], hx) * (1.0 / math.sqrt(XA_HEAD_DIM))).astype(MXU_DTYPE)
    outs = []
    for hd in range(XA_HEADS):
        lo, hi = hd * XA_HEAD_DIM, (hd + 1) * XA_HEAD_DIM
        sT = _dot(km_ref[0, :, lo:hi], qx[lo:hi])
        m = jnp.max(sT, axis=0, keepdims=True)
        p = jnp.exp(sT - m)
        l = jnp.sum(p, axis=0, keepdims=True)
        outs.append((_dot(vmT_ref[0, lo:hi, :], p.astype(MXU_DTYPE)) / l).astype(MXU_DTYPE))
    x2 = x1 + _dot(wo_ref[...], jnp.concatenate(outs, axis=0))
    x2_ref[0] = x2
    hp_ref[0] = _rmsnorm_cols(x2, gffn_ref[...]).astype(hp_ref.dtype)


def _merge(x, attnT, gattnT, sgpT, waoT, woutT, gxa, wqT, kmem, vmemT, woT, gffn, tm):
    b, s, d = x.shape
    m = kmem.shape[1]
    tok = pl.BlockSpec((1, d, tm), lambda i, j: (i, 0, j))
    wspec = _const_spec((d, d))
    col = _const_spec((d, 1))
    return pl.pallas_call(
        _merge_kernel,
        out_shape=(jax.ShapeDtypeStruct((b, d, s), jnp.float32), jax.ShapeDtypeStruct((b, d, s), MXU_DTYPE)),
        grid=(b, s // tm),
        in_specs=[pl.BlockSpec((1, tm, d), lambda i, j: (i, j, 0)), tok, tok, tok, wspec, wspec, col, wspec,
                  pl.BlockSpec((1, m, d), lambda i, j: (i, 0, 0)), pl.BlockSpec((1, d, m), lambda i, j: (i, 0, 0)),
                  wspec, col],
        out_specs=(tok, tok),
        compiler_params=_params(("parallel", "parallel"), 48),
        name="merge",
    )(x, attnT, gattnT, sgpT, waoT, woutT, gxa, wqT, kmem, vmemT, woT, gffn)


def _top_values(s, top_ref):
    cur = s
    for r in range(PEER_TOPK):
        mx = jnp.max(cur, axis=0, keepdims=True)
        top_ref[r:r + 1, :] = mx
        if r + 1 < PEER_TOPK:
            cur = jnp.where(cur == mx, NEG_INF, cur)


def _stair_candidates(v1, v2):
    half = PEER_TOPK // 2
    row8 = lax.broadcasted_iota(jnp.int32, (half, 1), 0)
    cands = [v1 + v2[0:1], v1[0:1] + v2[half:]]
    for bb in range(1, half):
        a_max = PEER_TOPK // (bb + 1) - 1
        c = v1[:half] + v2[bb:bb + 1]
        cands.append(jnp.where(row8 <= a_max, c, NEG_INF))
    return jnp.concatenate(cands, axis=0)


def _peer_sel_kernel(hp_ref, wq_ref, k1_ref, k2_ref, nb_ref, p1_ref, r2_ref, p2_ref, v1_s, v2_s):
    qp = _dot(wq_ref[...], hp_ref[0]).astype(MXU_DTYPE)
    for hd in range(PEER_HEADS):
        base = hd * PEER_D_KEY
        s1 = _dot(k1_ref[hd], qp[base:base + PEER_HALF])
        s2 = _dot(k2_ref[hd], qp[base + PEER_HALF:base + PEER_D_KEY])
        _top_values(s1, v1_s)
        _top_values(s2, v2_s)
        v1 = v1_s[...]
        v2 = v2_s[...]
        cand = _stair_candidates(v1, v2)
        cur = cand
        for r in range(PEER_TOPK):
            t = jnp.max(cur, axis=0, keepdims=True)
            if r + 1 < PEER_TOPK:
                cur = jnp.where(cur == t, NEG_INF, cur)
        m1 = v1[0:1]
        m2 = v2[0:1]
        z = jnp.sum(jnp.where(cand >= t, jnp.exp(cand - (m1 + m2)), 0.0), axis=0, keepdims=True)
        nb = jnp.zeros_like(s1)
        r2 = jnp.zeros_like(s2)
        for b in range(PEER_TOPK):
            vb = v2[b:b + 1]
            nb = nb + jnp.where(s1 + vb >= t, 1.0, 0.0)
            r2 = r2 + jnp.where(vb > s2, 1.0, 0.0)
        nb_ref[0, hd] = nb
        p1_ref[0, hd] = jnp.exp(s1 - m1) / z
        r2_ref[0, hd] = r2
        p2_ref[0, hd] = jnp.exp(s2 - m2)


def _peer_sel(hpT, wpqT, k1, k2, tm):
    b, d, s = hpT.shape
    sel = jax.ShapeDtypeStruct((b, PEER_HEADS, PEER_N_KEYS, s), jnp.float32)
    sel_spec = pl.BlockSpec((1, PEER_HEADS, PEER_N_KEYS, tm), lambda i, j: (i, 0, 0, j))
    return pl.pallas_call(
        _peer_sel_kernel,
        out_shape=(sel, sel, sel, sel),
        grid=(b, s // tm),
        in_specs=[pl.BlockSpec((1, d, tm), lambda i, j: (i, 0, j)), _const_spec(wpqT.shape),
                  _const_spec(k1.shape), _const_spec(k2.shape)],
        out_specs=(sel_spec, sel_spec, sel_spec, sel_spec),
        scratch_shapes=[pltpu.VMEM((PEER_TOPK, tm), jnp.float32), pltpu.VMEM((PEER_TOPK, tm), jnp.float32)],
        compiler_params=_params(("parallel", "parallel"), 48),
        name="peer_sel",
    )(hpT, wpqT, k1, k2)


def _peer_mix_kernel(hp_ref, ed_ref, euT_ref, nb_ref, p1_ref, r2_ref, p2_ref, x2_ref, gfin_ref, y_ref,
                     acc_ref, act_ref, w_ref):
    c = pl.program_id(2)
    n_c = pl.num_programs(2)
    tt = hp_ref.shape[2]
    n_i = ed_ref.shape[0] // PEER_N_KEYS
    jb = 2 * SUBLANES

    @pl.when(c == 0)
    def _():
        acc_ref[...] = jnp.zeros_like(acc_ref)

    act_ref[...] = _gelu(_dot(ed_ref[...], hp_ref[0]))

    def lane_block(nb, carry):
        ls = pl.ds(pl.multiple_of(nb * LANES, LANES), LANES)
        for il in range(n_i):
            accs = [jnp.zeros((jb, LANES), jnp.float32) for _ in range(PEER_N_KEYS // jb)]
            for hd in range(PEER_HEADS):
                bn = nb_ref[0, hd, il:il + 1, ls]
                bp = p1_ref[0, hd, il:il + 1, ls]
                for k in range(PEER_N_KEYS // jb):
                    r2v = r2_ref[0, hd, k * jb:(k + 1) * jb, ls]
                    p2v = p2_ref[0, hd, k * jb:(k + 1) * jb, ls]
                    accs[k] = accs[k] + bp * jnp.where(r2v < bn, p2v, 0.0)
            for k in range(PEER_N_KEYS // jb):
                rows = pl.ds(il * PEER_N_KEYS + k * jb, jb)
                w_ref[rows, ls] = (accs[k] * act_ref[rows, ls]).astype(w_ref.dtype)
        return carry

    lax.fori_loop(0, tt // LANES, lane_block, 0)

    acc_ref[...] += _dot(euT_ref[...], w_ref[...])

    @pl.when(c == n_c - 1)
    def _():
        x3 = x2_ref[0] + acc_ref[...]
        y_ref[0] = _rmsnorm_cols(x3, gfin_ref[...]).T.astype(y_ref.dtype)


def _peer_mix(hpT, e_down, e_upT, nb, p1, r2, p2, x2T, gfin, tt, ec):
    b, d, s = hpT.shape
    n_e = e_down.shape[0]
    n_i = ec // PEER_N_KEYS
    tok = lambda rows: pl.BlockSpec((1, rows, tt), lambda i, j, c: (i, 0, j))
    row_spec = pl.BlockSpec((1, PEER_HEADS, n_i, tt), lambda i, j, c: (i, 0, c, j))
    full_spec = pl.BlockSpec((1, PEER_HEADS, PEER_N_KEYS, tt), lambda i, j, c: (i, 0, 0, j))
    return pl.pallas_call(
        _peer_mix_kernel,
        out_shape=jax.ShapeDtypeStruct((b, s, d), jnp.float32),
        grid=(b, s // tt, n_e // ec),
        in_specs=[tok(d),
                  pl.BlockSpec((ec, d), lambda i, j, c: (c, 0)),
                  pl.BlockSpec((d, ec), lambda i, j, c: (0, c)),
                  row_spec, row_spec, full_spec, full_spec, tok(d), _const_spec((d, 1))],
        out_specs=pl.BlockSpec((1, tt, d), lambda i, j, c: (i, j, 0)),
        scratch_shapes=[pltpu.VMEM((d, tt), jnp.float32), pltpu.VMEM((ec, tt), jnp.float32),
                        pltpu.VMEM((ec, tt), MXU_DTYPE)],
        compiler_params=_params(("parallel", "parallel", "arbitrary"), 56),
        name="peer_mix",
    )(hpT, e_down, e_upT, nb, p1, r2, p2, x2T, gfin)


def _rope_tables(s):
    rows = s // GRID_W
    row_pos = jnp.repeat(jnp.arange(rows, dtype=jnp.float32), GRID_W)
    col_pos = jnp.tile(jnp.arange(GRID_W, dtype=jnp.float32), rows)
    n_freq = HEAD_DIM // 4
    inv_freq = ROPE_THETA ** (-jnp.arange(n_freq, dtype=jnp.float32) / n_freq)
    ang_r = inv_freq[:, None] * row_pos[None, :]
    ang_c = inv_freq[:, None] * col_pos[None, :]
    cos_t = jnp.concatenate([jnp.cos(ang_r), jnp.cos(ang_r), jnp.cos(ang_c), jnp.cos(ang_c)], axis=0)
    sin_t = jnp.concatenate([-jnp.sin(ang_r), jnp.sin(ang_r), -jnp.sin(ang_c), jnp.sin(ang_c)], axis=0)
    return cos_t, sin_t


def _tiles(s):
    return dict(inproj=256, attn=256, merge=256, sel=256, mix=min(512, s), experts=1024)


def _layer(x, kmem, vmemT, w):
    s = x.shape[1]
    t = _tiles(s)
    cos_t, sin_t = _rope_tables(s)
    qT, k, vT, gattnT, sgpT = _inproj(x, w["gmix"], w["w_inT"], w["gq"], w["gk"], cos_t, sin_t, w["gsg"],
                                      w["wsT"], w["bs"], w["wsgoT"], t["inproj"])
    attnT = _attention(qT, k, vT, t["attn"])
    x2T, hpT = _merge(x, attnT, gattnT, sgpT, w["waoT"], w["woutT"], w["gxa"], w["wqxT"], kmem, vmemT,
                      w["woxT"], w["gffn"], t["merge"])
    nb, p1, r2, p2 = _peer_sel(hpT, w["wpqT"], w["k1"], w["k2"], t["sel"])
    return _peer_mix(hpT, w["e_down"], w["e_upT"], nb, p1, r2, p2, x2T, w["gfin"], t["mix"], t["experts"])


def kernel(x_prompt, x_sample, mem_prompt, mem_sample, norm_mix_g, w_in, q_norm_g, k_norm_g, sg_norm_g, sg_w,
           sg_b, w_attn_o, w_sg_o, w_out, norm_xa_g, norm_mem_g, wq_xa, wkv_xa, wo_xa, norm_ffn_g, w_peer_q,
           peer_k1, peer_k2, expert_down, expert_up, final_norm_g):
    assert w_in.shape[0] == 1, "single-layer trunk"
    f32 = jnp.float32
    cast_t = lambda a: a.T.astype(MXU_DTYPE)
    col = lambda g: g.reshape(-1, 1).astype(f32)
    w = dict(
        gmix=norm_mix_g[0].reshape(1, -1), w_inT=cast_t(w_in[0]), gq=col(q_norm_g[0]), gk=col(k_norm_g[0]),
        gsg=col(sg_norm_g[0]), wsT=jnp.swapaxes(sg_w[0], 1, 2).astype(MXU_DTYPE),
        bs=sg_b[0].reshape(SG_GROUPS, 1, SG_CHUNK), wsgoT=cast_t(w_sg_o[0]),
        waoT=cast_t(w_attn_o[0]), woutT=cast_t(w_out[0]), gxa=col(norm_xa_g[0]), wqxT=cast_t(wq_xa[0]),
        woxT=cast_t(wo_xa[0]), gffn=col(norm_ffn_g[0]), wpqT=cast_t(w_peer_q[0]),
        k1=peer_k1[0].astype(MXU_DTYPE), k2=peer_k2[0].astype(MXU_DTYPE),
        e_down=expert_down[0].astype(MXU_DTYPE), e_upT=cast_t(expert_up[0]), gfin=col(final_norm_g),
    )
    wkv = wkv_xa[0]
    wk = wkv[:, :D_MODEL].astype(MXU_DTYPE)
    wvT = cast_t(wkv[:, D_MODEL:])
    gmem = norm_mem_g[0].reshape(1, -1)
    outs = []
    for x, mem in ((x_prompt, mem_prompt), (x_sample, mem_sample)):
        kmem, vmemT = _kv_mem(mem, gmem, wk, wvT)
        outs.append(_layer(x, kmem, vmemT, w))
    return tuple(outs)
```

```python
import functools
import math

import jax
import jax.numpy as jnp
from jax import lax
from jax.experimental import pallas as pl
from jax.experimental.pallas import tpu as pltpu

D_MODEL = 1024
GRID_W = 64
N_HEADS = 16
N_KV_HEADS = 4
HEAD_DIM = 64
KV_GROUP = N_HEADS // N_KV_HEADS
ATTN_Q_WIDTH = N_HEADS * HEAD_DIM
ATTN_KV_WIDTH = N_KV_HEADS * HEAD_DIM
ROPE_THETA = 10000.0
SG_WIDTH = 1024
SG_GROUPS = 8
SG_GROUP_DIM = SG_WIDTH // SG_GROUPS
SG_CHUNK = 128
XA_HEADS = 4
XA_HEAD_DIM = D_MODEL // XA_HEADS
PEER_HEADS = 8
PEER_N_KEYS = 128
PEER_D_KEY = 256
PEER_HALF = PEER_D_KEY // 2
PEER_TOPK = 16
EPS = 1e-6

_Q0 = 0
_K0 = _Q0 + ATTN_Q_WIDTH
_V0 = _K0 + ATTN_KV_WIDTH
_Z0 = _V0 + ATTN_KV_WIDTH
_G0 = _Z0 + 2 * SG_WIDTH
_IN_WIDTH = _G0 + 2 * D_MODEL

LANES = 128
SUBLANES = 8
V7X_VMEM_BYTES = 64 * 1024 * 1024

MXU_DTYPE = jnp.bfloat16
GATE_DTYPE = jnp.bfloat16
GATE_ROWS = 2 * SUBLANES
NEG_INF = float("-inf")


def _dot(a, b):
    return jnp.dot(a, b, preferred_element_type=jnp.float32)


def _dot_nt(a, b):
    return lax.dot_general(a, b, (((1,), (1,)), ((), ())), preferred_element_type=jnp.float32)


def _gelu(x):
    c = math.sqrt(2.0 / math.pi)
    return 0.5 * x * (1.0 + jnp.tanh(c * (x + 0.044715 * (x * x * x))))


def _sigmoid(x):
    return 1.0 / (1.0 + jnp.exp(-x))


def _rmsnorm_rows(x, g_row):
    ms = jnp.mean(x * x, axis=-1, keepdims=True)
    return x * lax.rsqrt(ms + EPS) * g_row


def _rmsnorm_cols(xT, g_col):
    ms = jnp.mean(xT * xT, axis=0, keepdims=True)
    return xT * lax.rsqrt(ms + EPS) * g_col


def _params(semantics, vmem_mb):
    return pltpu.CompilerParams(dimension_semantics=semantics, vmem_limit_bytes=vmem_mb * 1024 * 1024)


def _const_spec(shape):
    nd = len(shape)
    return pl.BlockSpec(shape, lambda *_: (0,) * nd)


def _kv_mem_kernel(mem_ref, g_ref, wk_ref, wvT_ref, k_ref, vT_ref):
    mn = _rmsnorm_rows(mem_ref[0], g_ref[...]).astype(MXU_DTYPE)
    k_ref[0] = _dot(mn, wk_ref[...]).astype(k_ref.dtype)
    vT_ref[0] = _dot_nt(wvT_ref[...], mn).astype(vT_ref.dtype)


def _kv_mem(mem, g_row, wk, wvT):
    nb, m, d = mem.shape
    return pl.pallas_call(
        _kv_mem_kernel,
        out_shape=(jax.ShapeDtypeStruct((nb, m, d), MXU_DTYPE), jax.ShapeDtypeStruct((nb, d, m), MXU_DTYPE)),
        grid=(nb,),
        in_specs=[pl.BlockSpec((1, m, d), lambda b: (b, 0, 0)), _const_spec((1, d)),
                  _const_spec((d, d)), _const_spec((d, d))],
        out_specs=(pl.BlockSpec((1, m, d), lambda b: (b, 0, 0)), pl.BlockSpec((1, d, m), lambda b: (b, 0, 0))),
        compiler_params=_params(("parallel",), 32),
        name="kv_mem",
    )(mem, g_row, wk, wvT)


def _head_norm_rope(t, g_col, cos, sin):
    ms = jnp.mean(t * t, axis=0, keepdims=True)
    t = t * lax.rsqrt(ms + EPS) * g_col
    q4 = HEAD_DIM // 4
    sw = jnp.concatenate([t[q4:2 * q4], t[0:q4], t[3 * q4:], t[2 * q4:3 * q4]], axis=0)
    return t * cos + sw * sin


def _inproj_kernel(x_ref, gmix_ref, w_ref, gq_ref, gk_ref, cos_ref, sin_ref, gsg_ref, wsT_ref, bs_ref, wsgo_ref,
                   qT_ref, k_ref, vT_ref, gattn_ref, sgp_ref):
    tm = x_ref.shape[1]
    h = _rmsnorm_rows(x_ref[0], gmix_ref[...]).astype(MXU_DTYPE)

    def proj_t(lo, hi):
        return _dot_nt(w_ref[lo:hi, :], h)

    cos = cos_ref[...]
    sin = sin_ref[...]

    q_t = proj_t(_Q0, _K0)
    scale = 1.0 / math.sqrt(HEAD_DIM)
    for hd in range(N_HEADS):
        r = _head_norm_rope(q_t[hd * HEAD_DIM:(hd + 1) * HEAD_DIM], gq_ref[...], cos, sin) * scale
        qT_ref[0, hd * HEAD_DIM:(hd + 1) * HEAD_DIM, :] = r.astype(qT_ref.dtype)

    k_t = proj_t(_K0, _V0)
    k_rot = jnp.concatenate(
        [_head_norm_rope(k_t[g * HEAD_DIM:(g + 1) * HEAD_DIM], gk_ref[...], cos, sin) for g in range(N_KV_HEADS)],
        axis=0)
    k_tok = k_rot.T
    for g in range(N_KV_HEADS):
        k_ref[0, g] = k_tok[:, g * HEAD_DIM:(g + 1) * HEAD_DIM].astype(k_ref.dtype)

    vT_ref[0] = proj_t(_V0, _Z0).astype(vT_ref.dtype)

    z_t = _gelu(proj_t(_Z0, _G0))
    u = z_t[:SG_WIDTH]
    vn = _rmsnorm_cols(z_t[SG_WIDTH:], gsg_ref[...]).astype(MXU_DTYPE)
    n_chunks = tm // SG_CHUNK
    sv_groups = []
    for g in range(SG_GROUPS):
        vg = vn[g * SG_GROUP_DIM:(g + 1) * SG_GROUP_DIM]
        lhs = jnp.concatenate([vg[:, c * SG_CHUNK:(c + 1) * SG_CHUNK] for c in range(n_chunks)], axis=0)
        r = _dot(lhs, wsT_ref[g]) + bs_ref[g]
        sv_groups.append(jnp.concatenate(
            [r[c * SG_GROUP_DIM:(c + 1) * SG_GROUP_DIM] for c in range(n_chunks)], axis=1))
    sg = (u * jnp.concatenate(sv_groups, axis=0)).astype(MXU_DTYPE)
    sg_branch = _dot(wsgo_ref[...], sg)

    gates = _sigmoid(proj_t(_G0, _IN_WIDTH))
    gattn_ref[0] = gates[:D_MODEL].astype(gattn_ref.dtype)
    sgp_ref[0] = (gates[D_MODEL:] * sg_branch).astype(sgp_ref.dtype)


def _inproj(x, gmix, w_inT, gq, gk, cos_t, sin_t, gsg, wsT, bs, wsgoT, tm):
    b, s, d = x.shape
    grid = (b, s // tm)
    tok = lambda rows: pl.BlockSpec((1, rows, tm), lambda i, j: (i, 0, j))
    return pl.pallas_call(
        _inproj_kernel,
        out_shape=(jax.ShapeDtypeStruct((b, ATTN_Q_WIDTH, s), MXU_DTYPE),
                   jax.ShapeDtypeStruct((b, N_KV_HEADS, s, HEAD_DIM), MXU_DTYPE),
                   jax.ShapeDtypeStruct((b, ATTN_KV_WIDTH, s), MXU_DTYPE),
                   jax.ShapeDtypeStruct((b, D_MODEL, s), MXU_DTYPE),
                   jax.ShapeDtypeStruct((b, D_MODEL, s), jnp.float32)),
        grid=grid,
        in_specs=[pl.BlockSpec((1, tm, d), lambda i, j: (i, j, 0)),
                  _const_spec((1, d)), _const_spec(w_inT.shape),
                  _const_spec((HEAD_DIM, 1)), _const_spec((HEAD_DIM, 1)),
                  pl.BlockSpec((HEAD_DIM, tm), lambda i, j: (0, j)),
                  pl.BlockSpec((HEAD_DIM, tm), lambda i, j: (0, j)),
                  _const_spec((SG_WIDTH, 1)), _const_spec(wsT.shape), _const_spec(bs.shape),
                  _const_spec(wsgoT.shape)],
        out_specs=(tok(ATTN_Q_WIDTH),
                   pl.BlockSpec((1, N_KV_HEADS, tm, HEAD_DIM), lambda i, j: (i, 0, j, 0)),
                   tok(ATTN_KV_WIDTH), tok(D_MODEL), tok(D_MODEL)),
        compiler_params=_params(("parallel", "parallel"), 56),
        name="inproj",
    )(x, gmix, w_inT, gq, gk, cos_t, sin_t, gsg, wsT, bs, wsgoT)


def _attn_kernel(qT_ref, k_ref, vT_ref, o_ref):
    k = k_ref[0, 0]
    vT = vT_ref[0]
    for a in range(KV_GROUP):
        qT = qT_ref[0, a * HEAD_DIM:(a + 1) * HEAD_DIM, :]
        sT = _dot(k, qT)
        m = jnp.max(sT, axis=0, keepdims=True)
        p = jnp.exp(sT - m)
        l = jnp.sum(p, axis=0, keepdims=True)
        oT = _dot(vT, p.astype(MXU_DTYPE)) / l
        o_ref[0, a * HEAD_DIM:(a + 1) * HEAD_DIM, :] = oT.astype(o_ref.dtype)


def _attention(qT, k, vT, tq):
    b, _, s = qT.shape
    rows = KV_GROUP * HEAD_DIM
    return pl.pallas_call(
        _attn_kernel,
        out_shape=jax.ShapeDtypeStruct((b, ATTN_Q_WIDTH, s), MXU_DTYPE),
        grid=(b, N_KV_HEADS, s // tq),
        in_specs=[pl.BlockSpec((1, rows, tq), lambda i, g, j: (i, g, j)),
                  pl.BlockSpec((1, 1, s, HEAD_DIM), lambda i, g, j: (i, g, 0, 0)),
                  pl.BlockSpec((1, HEAD_DIM, s), lambda i, g, j: (i, g, 0))],
        out_specs=pl.BlockSpec((1, rows, tq), lambda i, g, j: (i, g, j)),
        compiler_params=_params(("parallel", "parallel", "parallel"), 48),
        name="attn",
    )(qT, k, vT)


def _merge_kernel(x_ref, attn_ref, gattn_ref, sgp_ref, wao_ref, wout_ref, gxa_ref, wq_ref, km_ref, vmT_ref,
                  wo_ref, gffn_ref, x2_ref, hp_ref):
    xT = x_ref[0].T
    attn_branch = _dot(wao_ref[...], attn_ref[0])
    mix = gattn_ref[0].astype(jnp.float32) * attn_branch + sgp_ref[0]
    x1 = xT + _dot(wout_ref[...], mix.astype(MXU_DTYPE))

    hx = _rmsnorm_cols(x1, gxa_ref[...]).astype(MXU_DTYPE)
    qx = (_dot(wq_ref[...], hx) * (1.0 / math.sqrt(XA_HEAD_DIM))).astype(MXU_DTYPE)
    outs = []
    for hd in range(XA_HEADS):
        lo, hi = hd * XA_HEAD_DIM, (hd + 1) * XA_HEAD_DIM
        sT = _dot(km_ref[0, :, lo:hi], qx[lo:hi])
        m = jnp.max(sT, axis=0, keepdims=True)
        p = jnp.exp(sT - m)
        l = jnp.sum(p, axis=0, keepdims=True)
        outs.append((_dot(vmT_ref[0, lo:hi, :], p.astype(MXU_DTYPE)) / l).astype(MXU_DTYPE))
    x2 = x1 + _dot(wo_ref[...], jnp.concatenate(outs, axis=0))
    x2_ref[0] = x2
    hp_ref[0] = _rmsnorm_cols(x2, gffn_ref[...]).astype(hp_ref.dtype)


def _merge(x, attnT, gattnT, sgpT, waoT, woutT, gxa, wqT, kmem, vmemT, woT, gffn, tm):
    b, s, d = x.shape
    m = kmem.shape[1]
    tok = pl.BlockSpec((1, d, tm), lambda i, j: (i, 0, j))
    wspec = _const_spec((d, d))
    col = _const_spec((d, 1))
    return pl.pallas_call(
        _merge_kernel,
        out_shape=(jax.ShapeDtypeStruct((b, d, s), jnp.float32), jax.ShapeDtypeStruct((b, d, s), MXU_DTYPE)),
        grid=(b, s // tm),
        in_specs=[pl.BlockSpec((1, tm, d), lambda i, j: (i, j, 0)), tok, tok, tok, wspec, wspec, col, wspec,
                  pl.BlockSpec((1, m, d), lambda i, j: (i, 0, 0)), pl.BlockSpec((1, d, m), lambda i, j: (i, 0, 0)),
                  wspec, col],
        out_specs=(tok, tok),
        compiler_params=_params(("parallel", "parallel"), 48),
        name="merge",
    )(x, attnT, gattnT, sgpT, waoT, woutT, gxa, wqT, kmem, vmemT, woT, gffn)


def _top_values(s, top_ref):
    cur = s
    for r in range(PEER_TOPK):
        mx = jnp.max(cur, axis=0, keepdims=True)
        top_ref[r:r + 1, :] = mx
        if r + 1 < PEER_TOPK:
            cur = jnp.where(cur == mx, NEG_INF, cur)


def _stair_candidates(v1, v2):
    half = PEER_TOPK // 2
    row8 = lax.broadcasted_iota(jnp.int32, (half, 1), 0)
    cands = [v1 + v2[0:1], v1[0:1] + v2[half:]]
    for bb in range(1, half):
        a_max = PEER_TOPK // (bb + 1) - 1
        c = v1[:half] + v2[bb:bb + 1]
        cands.append(jnp.where(row8 <= a_max, c, NEG_INF))
    return jnp.concatenate(cands, axis=0)


def _peer_sel_kernel(hp_ref, wq_ref, k1_ref, k2_ref, nb_ref, p1_ref, r2_ref, p2_ref, v1_s, v2_s):
    qp = _dot(wq_ref[...], hp_ref[0]).astype(MXU_DTYPE)
    for hd in range(PEER_HEADS):
        base = hd * PEER_D_KEY
        s1 = _dot(k1_ref[hd], qp[base:base + PEER_HALF])
        s2 = _dot(k2_ref[hd], qp[base + PEER_HALF:base + PEER_D_KEY])
        _top_values(s1, v1_s)
        _top_values(s2, v2_s)
        v1 = v1_s[...]
        v2 = v2_s[...]
        cand = _stair_candidates(v1, v2)
        cur = cand
        for r in range(PEER_TOPK):
            t = jnp.max(cur, axis=0, keepdims=True)
            if r + 1 < PEER_TOPK:
                cur = jnp.where(cur == t, NEG_INF, cur)
        m1 = v1[0:1]
        m2 = v2[0:1]
        z = jnp.sum(jnp.where(cand >= t, jnp.exp(cand - (m1 + m2)), 0.0), axis=0, keepdims=True)
        nb = jnp.zeros_like(s1)
        r2 = jnp.zeros_like(s2)
        for b in range(PEER_TOPK):
            vb = v2[b:b + 1]
            nb = nb + jnp.where(s1 + vb >= t, 1.0, 0.0)
            r2 = r2 + jnp.where(vb > s2, 1.0, 0.0)
        nb_ref[0, hd] = _pair_words(nb)
        p1_ref[0, hd] = _pair_words(jnp.exp(s1 - m1) / z)
        r2_ref[0, hd] = r2.astype(r2_ref.dtype)
        p2_ref[0, hd] = jnp.exp(s2 - m2).astype(p2_ref.dtype)


def _pair_words(x):
    u = lax.bitcast_convert_type(x.astype(GATE_DTYPE).astype(jnp.float32), jnp.uint32)
    return u | (u >> 16)


def _peer_sel(hpT, wpqT, k1, k2, tm):
    b, d, s = hpT.shape
    shape = (b, PEER_HEADS, PEER_N_KEYS, s)
    words = jax.ShapeDtypeStruct(shape, jnp.uint32)
    sel = jax.ShapeDtypeStruct(shape, GATE_DTYPE)
    sel_spec = pl.BlockSpec((1, PEER_HEADS, PEER_N_KEYS, tm), lambda i, j: (i, 0, 0, j))
    return pl.pallas_call(
        _peer_sel_kernel,
        out_shape=(words, words, sel, sel),
        grid=(b, s // tm),
        in_specs=[pl.BlockSpec((1, d, tm), lambda i, j: (i, 0, j)), _const_spec(wpqT.shape),
                  _const_spec(k1.shape), _const_spec(k2.shape)],
        out_specs=(sel_spec, sel_spec, sel_spec, sel_spec),
        scratch_shapes=[pltpu.VMEM((PEER_TOPK, tm), jnp.float32), pltpu.VMEM((PEER_TOPK, tm), jnp.float32)],
        compiler_params=_params(("parallel", "parallel"), 48),
        name="peer_sel",
    )(hpT, wpqT, k1, k2)


def _peer_mix_kernel(hp_ref, ed_ref, euT_ref, nb_ref, p1_ref, r2_ref, p2_ref, x2_ref, gfin_ref, y_ref,
                     acc_ref, act_ref, w_ref, row_s, key_s):
    c = pl.program_id(2)
    n_c = pl.num_programs(2)
    tt = hp_ref.shape[2]
    n_i = ed_ref.shape[0] // PEER_N_KEYS
    jb = GATE_ROWS
    n_k = PEER_N_KEYS // jb

    @pl.when(c == 0)
    def _():
        acc_ref[...] = jnp.zeros_like(acc_ref)
        key_s[0] = r2_ref[0]
        key_s[1] = p2_ref[0]

    row_s[0] = nb_ref[0]
    row_s[1] = p1_ref[0]

    act_ref[...] = _gelu(_dot(ed_ref[...], hp_ref[0])).astype(act_ref.dtype)

    def row_bcast(words):
        return pltpu.bitcast(jnp.broadcast_to(words, (SUBLANES, LANES)), GATE_DTYPE)

    zero = jnp.zeros((jb, LANES), GATE_DTYPE)
    for lb in range(tt // LANES):
        ls = pl.ds(lb * LANES, LANES)
        for il in range(n_i):
            accs = [None] * n_k
            for hd in range(PEER_HEADS):
                bn = row_bcast(row_s[0, hd, il:il + 1, ls])
                bp = row_bcast(row_s[1, hd, il:il + 1, ls])
                for k in range(n_k):
                    r2v = key_s[0, hd, k * jb:(k + 1) * jb, ls]
                    p2v = key_s[1, hd, k * jb:(k + 1) * jb, ls]
                    term = bp * jnp.where(r2v < bn, p2v, zero)
                    accs[k] = term if accs[k] is None else accs[k] + term
            for k in range(n_k):
                rows = pl.ds(il * PEER_N_KEYS + k * jb, jb)
                w_ref[rows, ls] = (accs[k] * act_ref[rows, ls]).astype(w_ref.dtype)

    acc_ref[...] += _dot(euT_ref[...], w_ref[...])

    @pl.when(c == n_c - 1)
    def _():
        x3 = x2_ref[0] + acc_ref[...]
        y_ref[0] = _rmsnorm_cols(x3, gfin_ref[...]).T.astype(y_ref.dtype)


def _peer_mix(hpT, e_down, e_upT, nb, p1, r2, p2, x2T, gfin, tt, ec):
    b, d, s = hpT.shape
    n_e = e_down.shape[0]
    n_i = ec // PEER_N_KEYS
    tok = lambda rows: pl.BlockSpec((1, rows, tt), lambda i, j, c: (i, 0, j))
    row_spec = pl.BlockSpec((1, PEER_HEADS, n_i, tt), lambda i, j, c: (i, 0, c, j))
    full_spec = pl.BlockSpec((1, PEER_HEADS, PEER_N_KEYS, tt), lambda i, j, c: (i, 0, 0, j))
    return pl.pallas_call(
        _peer_mix_kernel,
        out_shape=jax.ShapeDtypeStruct((b, s, d), jnp.float32),
        grid=(b, s // tt, n_e // ec),
        in_specs=[tok(d),
                  pl.BlockSpec((ec, d), lambda i, j, c: (c, 0)),
                  pl.BlockSpec((d, ec), lambda i, j, c: (0, c)),
                  row_spec, row_spec, full_spec, full_spec, tok(d), _const_spec((d, 1))],
        out_specs=pl.BlockSpec((1, tt, d), lambda i, j, c: (i, j, 0)),
        scratch_shapes=[pltpu.VMEM((d, tt), jnp.float32), pltpu.VMEM((ec, tt), GATE_DTYPE),
                        pltpu.VMEM((ec, tt), MXU_DTYPE),
                        pltpu.VMEM((2, PEER_HEADS, n_i, tt), jnp.uint32),
                        pltpu.VMEM((2, PEER_HEADS, PEER_N_KEYS, tt), GATE_DTYPE)],
        compiler_params=_params(("parallel", "parallel", "arbitrary"), 56),
        name="peer_mix",
    )(hpT, e_down, e_upT, nb, p1, r2, p2, x2T, gfin)


def _rope_tables(s):
    rows = s // GRID_W
    row_pos = jnp.repeat(jnp.arange(rows, dtype=jnp.float32), GRID_W)
    col_pos = jnp.tile(jnp.arange(GRID_W, dtype=jnp.float32), rows)
    n_freq = HEAD_DIM // 4
    inv_freq = ROPE_THETA ** (-jnp.arange(n_freq, dtype=jnp.float32) / n_freq)
    ang_r = inv_freq[:, None] * row_pos[None, :]
    ang_c = inv_freq[:, None] * col_pos[None, :]
    cos_t = jnp.concatenate([jnp.cos(ang_r), jnp.cos(ang_r), jnp.cos(ang_c), jnp.cos(ang_c)], axis=0)
    sin_t = jnp.concatenate([-jnp.sin(ang_r), jnp.sin(ang_r), -jnp.sin(ang_c), jnp.sin(ang_c)], axis=0)
    return cos_t, sin_t


def _tiles(s):
    return dict(inproj=256, attn=256, merge=256, sel=256, mix=min(512, s), experts=1024)


def _layer(x, kmem, vmemT, w):
    s = x.shape[1]
    t = _tiles(s)
    cos_t, sin_t = _rope_tables(s)
    qT, k, vT, gattnT, sgpT = _inproj(x, w["gmix"], w["w_inT"], w["gq"], w["gk"], cos_t, sin_t, w["gsg"],
                                      w["wsT"], w["bs"], w["wsgoT"], t["inproj"])
    attnT = _attention(qT, k, vT, t["attn"])
    x2T, hpT = _merge(x, attnT, gattnT, sgpT, w["waoT"], w["woutT"], w["gxa"], w["wqxT"], kmem, vmemT,
                      w["woxT"], w["gffn"], t["merge"])
    nb, p1, r2, p2 = _peer_sel(hpT, w["wpqT"], w["k1"], w["k2"], t["sel"])
    return _peer_mix(hpT, w["e_down"], w["e_upT"], nb, p1, r2, p2, x2T, w["gfin"], t["mix"], t["experts"])


def kernel(x_prompt, x_sample, mem_prompt, mem_sample, norm_mix_g, w_in, q_norm_g, k_norm_g, sg_norm_g, sg_w,
           sg_b, w_attn_o, w_sg_o, w_out, norm_xa_g, norm_mem_g, wq_xa, wkv_xa, wo_xa, norm_ffn_g, w_peer_q,
           peer_k1, peer_k2, expert_down, expert_up, final_norm_g):
    assert w_in.shape[0] == 1, "single-layer trunk"
    f32 = jnp.float32
    cast_t = lambda a: a.T.astype(MXU_DTYPE)
    col = lambda g: g.reshape(-1, 1).astype(f32)
    w = dict(
        gmix=norm_mix_g[0].reshape(1, -1), w_inT=cast_t(w_in[0]), gq=col(q_norm_g[0]), gk=col(k_norm_g[0]),
        gsg=col(sg_norm_g[0]), wsT=jnp.swapaxes(sg_w[0], 1, 2).astype(MXU_DTYPE),
        bs=sg_b[0].reshape(SG_GROUPS, 1, SG_CHUNK), wsgoT=cast_t(w_sg_o[0]),
        waoT=cast_t(w_attn_o[0]), woutT=cast_t(w_out[0]), gxa=col(norm_xa_g[0]), wqxT=cast_t(wq_xa[0]),
        woxT=cast_t(wo_xa[0]), gffn=col(norm_ffn_g[0]), wpqT=cast_t(w_peer_q[0]),
        k1=peer_k1[0].astype(MXU_DTYPE), k2=peer_k2[0].astype(MXU_DTYPE),
        e_down=expert_down[0].astype(MXU_DTYPE), e_upT=cast_t(expert_up[0]), gfin=col(final_norm_g),
    )
    wkv = wkv_xa[0]
    wk = wkv[:, :D_MODEL].astype(MXU_DTYPE)
    wvT = cast_t(wkv[:, D_MODEL:])
    gmem = norm_mem_g[0].reshape(1, -1)
    outs = []
    for x, mem in ((x_prompt, mem_prompt), (x_sample, mem_sample)):
        kmem, vmemT = _kv_mem(mem, gmem, wk, wvT)
        outs.append(_layer(x, kmem, vmemT, w))
    return tuple(outs)
```

```python
import functools
import math

import jax
import jax.numpy as jnp
from jax import lax
from jax.experimental import pallas as pl
from jax.experimental.pallas import tpu as pltpu

D_MODEL = 1024
GRID_W = 64
N_HEADS = 16
N_KV_HEADS = 4
HEAD_DIM = 64
KV_GROUP = N_HEADS // N_KV_HEADS
ATTN_Q_WIDTH = N_HEADS * HEAD_DIM
ATTN_KV_WIDTH = N_KV_HEADS * HEAD_DIM
ROPE_THETA = 10000.0
SG_WIDTH = 1024
SG_GROUPS = 8
SG_GROUP_DIM = SG_WIDTH // SG_GROUPS
SG_CHUNK = 128
XA_HEADS = 4
XA_HEAD_DIM = D_MODEL // XA_HEADS
PEER_HEADS = 8
PEER_N_KEYS = 128
PEER_D_KEY = 256
PEER_HALF = PEER_D_KEY // 2
PEER_TOPK = 16
PEER_CHUNK = 1024
EPS = 1e-6

_Q0 = 0
_K0 = _Q0 + ATTN_Q_WIDTH
_V0 = _K0 + ATTN_KV_WIDTH
_Z0 = _V0 + ATTN_KV_WIDTH
_G0 = _Z0 + 2 * SG_WIDTH
_IN_WIDTH = _G0 + 2 * D_MODEL

LANES = 128
SUBLANES = 8
V7X_VMEM_BYTES = 64 * 1024 * 1024

MXU_DTYPE = jnp.bfloat16
GATE_DTYPE = jnp.bfloat16
GATE_ROWS = 2 * SUBLANES
NEG_INF = float("-inf")


def _dot(a, b):
    return jnp.dot(a, b, preferred_element_type=jnp.float32)


def _dot_nt(a, b):
    return lax.dot_general(a, b, (((1,), (1,)), ((), ())), preferred_element_type=jnp.float32)


def _gelu(x):
    c = math.sqrt(2.0 / math.pi)
    return 0.5 * x * (1.0 + jnp.tanh(c * (x + 0.044715 * (x * x * x))))


def _sigmoid(x):
    return 1.0 / (1.0 + jnp.exp(-x))


def _rmsnorm_rows(x, g_row):
    ms = jnp.mean(x * x, axis=-1, keepdims=True)
    return x * lax.rsqrt(ms + EPS) * g_row


def _rmsnorm_cols(xT, g_col):
    ms = jnp.mean(xT * xT, axis=0, keepdims=True)
    return xT * lax.rsqrt(ms + EPS) * g_col


def _params(semantics, vmem_mb):
    return pltpu.CompilerParams(dimension_semantics=semantics, vmem_limit_bytes=vmem_mb * 1024 * 1024)


def _const_spec(shape):
    nd = len(shape)
    return pl.BlockSpec(shape, lambda *_: (0,) * nd)


def _kv_mem_kernel(mem_ref, g_ref, wk_ref, wvT_ref, k_ref, vT_ref):
    mn = _rmsnorm_rows(mem_ref[0], g_ref[...]).astype(MXU_DTYPE)
    k_ref[0] = _dot(mn, wk_ref[...]).astype(k_ref.dtype)
    vT_ref[0] = _dot_nt(wvT_ref[...], mn).astype(vT_ref.dtype)


def _kv_mem(mem, g_row, wk, wvT):
    nb, m, d = mem.shape
    return pl.pallas_call(
        _kv_mem_kernel,
        out_shape=(jax.ShapeDtypeStruct((nb, m, d), MXU_DTYPE), jax.ShapeDtypeStruct((nb, d, m), MXU_DTYPE)),
        grid=(nb,),
        in_specs=[pl.BlockSpec((1, m, d), lambda b: (b, 0, 0)), _const_spec((1, d)),
                  _const_spec((d, d)), _const_spec((d, d))],
        out_specs=(pl.BlockSpec((1, m, d), lambda b: (b, 0, 0)), pl.BlockSpec((1, d, m), lambda b: (b, 0, 0))),
        compiler_params=_params(("parallel",), 32),
        name="kv_mem",
    )(mem, g_row, wk, wvT)


def _head_norm_rope(t, g_col, cos, sin):
    ms = jnp.mean(t * t, axis=0, keepdims=True)
    t = t * lax.rsqrt(ms + EPS) * g_col
    q4 = HEAD_DIM // 4
    sw = jnp.concatenate([t[q4:2 * q4], t[0:q4], t[3 * q4:], t[2 * q4:3 * q4]], axis=0)
    return t * cos + sw * sin


def _inproj_kernel(x_ref, gmix_ref, w_ref, gq_ref, gk_ref, cos_ref, sin_ref, gsg_ref, wsT_ref, bs_ref, wsgo_ref,
                   qT_ref, k_ref, vT_ref, gattn_ref, sgp_ref):
    tm = x_ref.shape[1]
    h = _rmsnorm_rows(x_ref[0], gmix_ref[...]).astype(MXU_DTYPE)

    def proj_t(lo, hi):
        return _dot_nt(w_ref[lo:hi, :], h)

    cos = cos_ref[...]
    sin = sin_ref[...]

    q_t = proj_t(_Q0, _K0)
    scale = math.log2(math.e) / math.sqrt(HEAD_DIM)
    for hd in range(N_HEADS):
        r = _head_norm_rope(q_t[hd * HEAD_DIM:(hd + 1) * HEAD_DIM], gq_ref[...], cos, sin) * scale
        qT_ref[0, hd * HEAD_DIM:(hd + 1) * HEAD_DIM, :] = r.astype(qT_ref.dtype)

    k_t = proj_t(_K0, _V0)
    k_rot = jnp.concatenate(
        [_head_norm_rope(k_t[g * HEAD_DIM:(g + 1) * HEAD_DIM], gk_ref[...], cos, sin) for g in range(N_KV_HEADS)],
        axis=0)
    k_tok = k_rot.T
    for g in range(N_KV_HEADS):
        k_ref[0, g] = k_tok[:, g * HEAD_DIM:(g + 1) * HEAD_DIM].astype(k_ref.dtype)

    vT_ref[0] = proj_t(_V0, _Z0).astype(vT_ref.dtype)

    z_t = _gelu(proj_t(_Z0, _G0))
    u = z_t[:SG_WIDTH]
    vn = _rmsnorm_cols(z_t[SG_WIDTH:], gsg_ref[...]).astype(MXU_DTYPE)
    n_chunks = tm // SG_CHUNK
    sv_groups = []
    for g in range(SG_GROUPS):
        vg = vn[g * SG_GROUP_DIM:(g + 1) * SG_GROUP_DIM]
        lhs = jnp.concatenate([vg[:, c * SG_CHUNK:(c + 1) * SG_CHUNK] for c in range(n_chunks)], axis=0)
        r = _dot(lhs, wsT_ref[g]) + bs_ref[g]
        sv_groups.append(jnp.concatenate(
            [r[c * SG_GROUP_DIM:(c + 1) * SG_GROUP_DIM] for c in range(n_chunks)], axis=1))
    sg = (u * jnp.concatenate(sv_groups, axis=0)).astype(MXU_DTYPE)
    sg_branch = _dot(wsgo_ref[...], sg)

    gates = _sigmoid(proj_t(_G0, _IN_WIDTH))
    gattn_ref[0] = gates[:D_MODEL].astype(gattn_ref.dtype)
    sgp_ref[0] = (gates[D_MODEL:] * sg_branch).astype(sgp_ref.dtype)


def _inproj(x, gmix, w_inT, gq, gk, cos_t, sin_t, gsg, wsT, bs, wsgoT, tm):
    b, s, d = x.shape
    grid = (b, s // tm)
    tok = lambda rows: pl.BlockSpec((1, rows, tm), lambda i, j: (i, 0, j))
    return pl.pallas_call(
        _inproj_kernel,
        out_shape=(jax.ShapeDtypeStruct((b, ATTN_Q_WIDTH, s), MXU_DTYPE),
                   jax.ShapeDtypeStruct((b, N_KV_HEADS, s, HEAD_DIM), MXU_DTYPE),
                   jax.ShapeDtypeStruct((b, ATTN_KV_WIDTH, s), MXU_DTYPE),
                   jax.ShapeDtypeStruct((b, D_MODEL, s), MXU_DTYPE),
                   jax.ShapeDtypeStruct((b, D_MODEL, s), jnp.float32)),
        grid=grid,
        in_specs=[pl.BlockSpec((1, tm, d), lambda i, j: (i, j, 0)),
                  _const_spec((1, d)), _const_spec(w_inT.shape),
                  _const_spec((HEAD_DIM, 1)), _const_spec((HEAD_DIM, 1)),
                  pl.BlockSpec((HEAD_DIM, tm), lambda i, j: (0, j)),
                  pl.BlockSpec((HEAD_DIM, tm), lambda i, j: (0, j)),
                  _const_spec((SG_WIDTH, 1)), _const_spec(wsT.shape), _const_spec(bs.shape),
                  _const_spec(wsgoT.shape)],
        out_specs=(tok(ATTN_Q_WIDTH),
                   pl.BlockSpec((1, N_KV_HEADS, tm, HEAD_DIM), lambda i, j: (i, 0, j, 0)),
                   tok(ATTN_KV_WIDTH), tok(D_MODEL), tok(D_MODEL)),
        compiler_params=_params(("parallel", "parallel"), 56),
        name="inproj",
    )(x, gmix, w_inT, gq, gk, cos_t, sin_t, gsg, wsT, bs, wsgoT)


def _attn_kernel(qT_ref, k_ref, vT_ref, o_ref):
    tq = qT_ref.shape[2]
    k = k_ref[0, 0]
    vT = vT_ref[0]
    vT1 = jnp.concatenate([vT, jnp.ones((GATE_ROWS, vT.shape[1]), vT.dtype)], axis=0)
    for pair in range(KV_GROUP // 2):
        heads = (2 * pair, 2 * pair + 1)
        qT2 = jnp.concatenate([qT_ref[0, a * HEAD_DIM:(a + 1) * HEAD_DIM, :] for a in heads], axis=1)
        sT = _dot(k, qT2).astype(MXU_DTYPE)
        m = jnp.max(sT, axis=0, keepdims=True)
        p = jnp.exp2(sT - m)
        o = _dot(vT1, p)
        oT = o[:HEAD_DIM] / o[HEAD_DIM:HEAD_DIM + 1]
        for i, a in enumerate(heads):
            o_ref[0, a * HEAD_DIM:(a + 1) * HEAD_DIM, :] = oT[:, i * tq:(i + 1) * tq].astype(o_ref.dtype)


def _attention(qT, k, vT, tq):
    b, _, s = qT.shape
    rows = KV_GROUP * HEAD_DIM
    return pl.pallas_call(
        _attn_kernel,
        out_shape=jax.ShapeDtypeStruct((b, ATTN_Q_WIDTH, s), MXU_DTYPE),
        grid=(b, N_KV_HEADS, s // tq),
        in_specs=[pl.BlockSpec((1, rows, tq), lambda i, g, j: (i, g, j)),
                  pl.BlockSpec((1, 1, s, HEAD_DIM), lambda i, g, j: (i, g, 0, 0)),
                  pl.BlockSpec((1, HEAD_DIM, s), lambda i, g, j: (i, g, 0))],
        out_specs=pl.BlockSpec((1, rows, tq), lambda i, g, j: (i, g, j)),
        compiler_params=_params(("parallel", "parallel", "parallel"), 48),
        name="attn",
    )(qT, k, vT)


def _merge_kernel(x_ref, attn_ref, gattn_ref, sgp_ref, wao_ref, wout_ref, gxa_ref, wq_ref, km_ref, vmT_ref,
                  wo_ref, gffn_ref, x2_ref, hp_ref):
    xT = x_ref[0].T
    attn_branch = _dot(wao_ref[...], attn_ref[0])
    mix = gattn_ref[0].astype(jnp.float32) * attn_branch + sgp_ref[0]
    x1 = xT + _dot(wout_ref[...], mix.astype(MXU_DTYPE))

    hx = _rmsnorm_cols(x1, gxa_ref[...]).astype(MXU_DTYPE)
    qx = (_dot(wq_ref[...], hx) * (1.0 / math.sqrt(XA_HEAD_DIM))).astype(MXU_DTYPE)
    outs = []
    for hd in range(XA_HEADS):
        lo, hi = hd * XA_HEAD_DIM, (hd + 1) * XA_HEAD_DIM
        sT = _dot(km_ref[0, :, lo:hi], qx[lo:hi])
        m = jnp.max(sT, axis=0, keepdims=True)
        p = jnp.exp(sT - m)
        l = jnp.sum(p, axis=0, keepdims=True)
        outs.append((_dot(vmT_ref[0, lo:hi, :], p.astype(MXU_DTYPE)) / l).astype(MXU_DTYPE))
    x2 = x1 + _dot(wo_ref[...], jnp.concatenate(outs, axis=0))
    x2_ref[0] = x2
    hp_ref[0] = _rmsnorm_cols(x2, gffn_ref[...]).astype(hp_ref.dtype)


def _merge(x, attnT, gattnT, sgpT, waoT, woutT, gxa, wqT, kmem, vmemT, woT, gffn, tm):
    b, s, d = x.shape
    m = kmem.shape[1]
    tok = pl.BlockSpec((1, d, tm), lambda i, j: (i, 0, j))
    wspec = _const_spec((d, d))
    col = _const_spec((d, 1))
    return pl.pallas_call(
        _merge_kernel,
        out_shape=(jax.ShapeDtypeStruct((b, d, s), jnp.float32), jax.ShapeDtypeStruct((b, d, s), MXU_DTYPE)),
        grid=(b, s // tm),
        in_specs=[pl.BlockSpec((1, tm, d), lambda i, j: (i, j, 0)), tok, tok, tok, wspec, wspec, col, wspec,
                  pl.BlockSpec((1, m, d), lambda i, j: (i, 0, 0)), pl.BlockSpec((1, d, m), lambda i, j: (i, 0, 0)),
                  wspec, col],
        out_specs=(tok, tok),
        compiler_params=_params(("parallel", "parallel"), 48),
        name="merge",
    )(x, attnT, gattnT, sgpT, waoT, woutT, gxa, wqT, kmem, vmemT, woT, gffn)


def _top_values(s, top_ref):
    cur = s
    for r in range(PEER_TOPK):
        mx = jnp.max(cur, axis=0, keepdims=True)
        top_ref[r:r + 1, :] = mx
        if r + 1 < PEER_TOPK:
            cur = jnp.where(cur == mx, NEG_INF, cur)


def _stair_candidates(v1, v2):
    half = PEER_TOPK // 2
    row8 = lax.broadcasted_iota(jnp.int32, (half, 1), 0)
    cands = [v1 + v2[0:1], v1[0:1] + v2[half:]]
    for bb in range(1, half):
        a_max = PEER_TOPK // (bb + 1) - 1
        c = v1[:half] + v2[bb:bb + 1]
        cands.append(jnp.where(row8 <= a_max, c, NEG_INF))
    return jnp.concatenate(cands, axis=0)


def _peer_sel_kernel(hp_ref, wq_ref, k1_ref, k2_ref, nb_ref, p1_ref, r2_ref, p2_ref, v1_s, v2_s):
    qp = _dot(wq_ref[...], hp_ref[0]).astype(MXU_DTYPE)
    for hd in range(PEER_HEADS):
        base = hd * PEER_D_KEY
        s1 = _dot(k1_ref[hd], qp[base:base + PEER_HALF])
        s2 = _dot(k2_ref[hd], qp[base + PEER_HALF:base + PEER_D_KEY])
        _top_values(s1, v1_s)
        _top_values(s2, v2_s)
        v1 = v1_s[...]
        v2 = v2_s[...]
        cand = _stair_candidates(v1, v2)
        cur = cand
        for r in range(PEER_TOPK):
            t = jnp.max(cur, axis=0, keepdims=True)
            if r + 1 < PEER_TOPK:
                cur = jnp.where(cur == t, NEG_INF, cur)
        m1 = v1[0:1]
        m2 = v2[0:1]
        z = jnp.sum(jnp.where(cand >= t, jnp.exp(cand - (m1 + m2)), 0.0), axis=0, keepdims=True)
        nb = jnp.zeros_like(s1)
        r2 = jnp.zeros_like(s2)
        for b in range(PEER_TOPK):
            vb = v2[b:b + 1]
            nb = nb + jnp.where(s1 + vb >= t, 1.0, 0.0)
            r2 = r2 + jnp.where(vb > s2, 1.0, 0.0)
        nb_ref[0, hd] = _pair_words(nb)
        p1_ref[0, hd] = _pair_words(jnp.exp(s1 - m1) / z)
        r2_ref[0, hd] = r2.astype(r2_ref.dtype)
        p2_ref[0, hd] = jnp.exp(s2 - m2).astype(p2_ref.dtype)


def _pair_words(x):
    u = lax.bitcast_convert_type(x.astype(GATE_DTYPE).astype(jnp.float32), jnp.uint32)
    return u | (u >> 16)


def _peer_sel(hpT, wpqT, k1, k2, tm):
    b, d, s = hpT.shape
    shape = (b, PEER_HEADS, PEER_N_KEYS, s)
    words = jax.ShapeDtypeStruct(shape, jnp.uint32)
    sel = jax.ShapeDtypeStruct(shape, GATE_DTYPE)
    sel_spec = pl.BlockSpec((1, PEER_HEADS, PEER_N_KEYS, tm), lambda i, j: (i, 0, 0, j))
    return pl.pallas_call(
        _peer_sel_kernel,
        out_shape=(words, words, sel, sel),
        grid=(b, s // tm),
        in_specs=[pl.BlockSpec((1, d, tm), lambda i, j: (i, 0, j)), _const_spec(wpqT.shape),
                  _const_spec(k1.shape), _const_spec(k2.shape)],
        out_specs=(sel_spec, sel_spec, sel_spec, sel_spec),
        scratch_shapes=[pltpu.VMEM((PEER_TOPK, tm), jnp.float32), pltpu.VMEM((PEER_TOPK, tm), jnp.float32)],
        compiler_params=_params(("parallel", "parallel"), 48),
        name="peer_sel",
    )(hpT, wpqT, k1, k2)


def _peer_mix_kernel(hp_ref, ed_ref, euT_ref, nb_ref, p1_ref, r2_ref, p2_ref, x2_ref, gfin_ref, y_ref,
                     acc_ref, act_ref, w_ref, row_s, key_s):
    c = pl.program_id(2)
    n_c = pl.num_programs(2)
    tt = hp_ref.shape[2]
    n_i = ed_ref.shape[0] // PEER_N_KEYS
    jb = GATE_ROWS
    n_k = PEER_N_KEYS // jb

    @pl.when(c == 0)
    def _():
        acc_ref[...] = jnp.zeros_like(acc_ref)
        key_s[0] = r2_ref[0]
        key_s[1] = p2_ref[0]

    row_s[0] = nb_ref[0]
    row_s[1] = p1_ref[0]

    act_ref[...] = _gelu(_dot(ed_ref[...], hp_ref[0])).astype(act_ref.dtype)

    def row_bcast(words):
        return pltpu.bitcast(jnp.broadcast_to(words, (SUBLANES, LANES)), GATE_DTYPE)

    zero = jnp.zeros((jb, LANES), GATE_DTYPE)
    for lb in range(tt // LANES):
        ls = pl.ds(lb * LANES, LANES)
        for il in range(n_i):
            accs = [None] * n_k
            for hd in range(PEER_HEADS):
                bn = row_bcast(row_s[0, hd, il:il + 1, ls])
                bp = row_bcast(row_s[1, hd, il:il + 1, ls])
                for k in range(n_k):
                    r2v = key_s[0, hd, k * jb:(k + 1) * jb, ls]
                    p2v = key_s[1, hd, k * jb:(k + 1) * jb, ls]
                    term = bp * jnp.where(r2v < bn, p2v, zero)
                    accs[k] = term if accs[k] is None else accs[k] + term
            for k in range(n_k):
                rows = pl.ds(il * PEER_N_KEYS + k * jb, jb)
                w_ref[rows, ls] = (accs[k] * act_ref[rows, ls]).astype(w_ref.dtype)

    acc_ref[...] += _dot(euT_ref[0], w_ref[...])

    @pl.when(c == n_c - 1)
    def _():
        x3 = x2_ref[0] + acc_ref[...]
        y_ref[0] = _rmsnorm_cols(x3, gfin_ref[...]).T.astype(y_ref.dtype)


def _peer_mix(hpT, e_down, e_upT, nb, p1, r2, p2, x2T, gfin, tt):
    b, d, s = hpT.shape
    n_e = e_down.shape[0]
    ec = PEER_CHUNK
    n_i = ec // PEER_N_KEYS
    tok = lambda rows: pl.BlockSpec((1, rows, tt), lambda i, j, c: (i, 0, j))
    row_spec = pl.BlockSpec((1, PEER_HEADS, n_i, tt), lambda i, j, c: (i, 0, c, j))
    full_spec = pl.BlockSpec((1, PEER_HEADS, PEER_N_KEYS, tt), lambda i, j, c: (i, 0, 0, j))
    return pl.pallas_call(
        _peer_mix_kernel,
        out_shape=jax.ShapeDtypeStruct((b, s, d), jnp.float32),
        grid=(b, s // tt, n_e // ec),
        in_specs=[tok(d),
                  pl.BlockSpec((ec, d), lambda i, j, c: (c, 0)),
                  pl.BlockSpec((1, d, ec), lambda i, j, c: (c, 0, 0)),
                  row_spec, row_spec, full_spec, full_spec, tok(d), _const_spec((d, 1))],
        out_specs=pl.BlockSpec((1, tt, d), lambda i, j, c: (i, j, 0)),
        scratch_shapes=[pltpu.VMEM((d, tt), jnp.float32), pltpu.VMEM((ec, tt), GATE_DTYPE),
                        pltpu.VMEM((ec, tt), MXU_DTYPE),
                        pltpu.VMEM((2, PEER_HEADS, n_i, tt), jnp.uint32),
                        pltpu.VMEM((2, PEER_HEADS, PEER_N_KEYS, tt), GATE_DTYPE)],
        compiler_params=_params(("parallel", "parallel", "arbitrary"), 56),
        name="peer_mix",
    )(hpT, e_down, e_upT, nb, p1, r2, p2, x2T, gfin)


def _rope_tables(s):
    rows = s // GRID_W
    row_pos = jnp.repeat(jnp.arange(rows, dtype=jnp.float32), GRID_W)
    col_pos = jnp.tile(jnp.arange(GRID_W, dtype=jnp.float32), rows)
    n_freq = HEAD_DIM // 4
    inv_freq = ROPE_THETA ** (-jnp.arange(n_freq, dtype=jnp.float32) / n_freq)
    ang_r = inv_freq[:, None] * row_pos[None, :]
    ang_c = inv_freq[:, None] * col_pos[None, :]
    cos_t = jnp.concatenate([jnp.cos(ang_r), jnp.cos(ang_r), jnp.cos(ang_c), jnp.cos(ang_c)], axis=0)
    sin_t = jnp.concatenate([-jnp.sin(ang_r), jnp.sin(ang_r), -jnp.sin(ang_c), jnp.sin(ang_c)], axis=0)
    return cos_t, sin_t


def _tiles(s):
    return dict(inproj=256, attn=256, merge=256, sel=256, mix=min(512, s))


def _layer(x, kmem, vmemT, w):
    s = x.shape[1]
    t = _tiles(s)
    cos_t, sin_t = _rope_tables(s)
    qT, k, vT, gattnT, sgpT = _inproj(x, w["gmix"], w["w_inT"], w["gq"], w["gk"], cos_t, sin_t, w["gsg"],
                                      w["wsT"], w["bs"], w["wsgoT"], t["inproj"])
    attnT = _attention(qT, k, vT, t["attn"])
    x2T, hpT = _merge(x, attnT, gattnT, sgpT, w["waoT"], w["woutT"], w["gxa"], w["wqxT"], kmem, vmemT,
                      w["woxT"], w["gffn"], t["merge"])
    nb, p1, r2, p2 = _peer_sel(hpT, w["wpqT"], w["k1"], w["k2"], t["sel"])
    return _peer_mix(hpT, w["e_down"], w["e_upT"], nb, p1, r2, p2, x2T, w["gfin"], t["mix"])


def kernel(x_prompt, x_sample, mem_prompt, mem_sample, norm_mix_g, w_in, q_norm_g, k_norm_g, sg_norm_g, sg_w,
           sg_b, w_attn_o, w_sg_o, w_out, norm_xa_g, norm_mem_g, wq_xa, wkv_xa, wo_xa, norm_ffn_g, w_peer_q,
           peer_k1, peer_k2, expert_down, expert_up, final_norm_g):
    assert w_in.shape[0] == 1, "single-layer trunk"
    w = _prep_weights(norm_mix_g[0], w_in[0], q_norm_g[0], k_norm_g[0], sg_norm_g[0], sg_w[0], sg_b[0],
                      w_attn_o[0], w_sg_o[0], w_out[0], norm_xa_g[0], norm_mem_g[0], wq_xa[0], wkv_xa[0],
                      wo_xa[0], norm_ffn_g[0], w_peer_q[0], peer_k1[0], peer_k2[0], expert_down[0],
                      expert_up[0], final_norm_g)
    outs = []
    for x, mem in ((x_prompt, mem_prompt), (x_sample, mem_sample)):
        kmem, vmemT = _kv_mem(mem, w["gmem"], w["wk"], w["wvT"])
        outs.append(_layer(x, kmem, vmemT, w))
    return tuple(outs)


def _prep_weights(norm_mix_g, w_in, q_norm_g, k_norm_g, sg_norm_g, sg_w, sg_b, w_attn_o, w_sg_o, w_out,
                  norm_xa_g, norm_mem_g, wq_xa, wkv_xa, wo_xa, norm_ffn_g, w_peer_q, peer_k1, peer_k2,
                  expert_down, expert_up, final_norm_g):
    cast_t = lambda a: a.T.astype(MXU_DTYPE)
    col = lambda g: g.reshape(-1, 1).astype(jnp.float32)
    n_e, d = expert_up.shape
    e_upT = jnp.swapaxes(expert_up.reshape(n_e // PEER_CHUNK, PEER_CHUNK, d), 1, 2).astype(MXU_DTYPE)
    return dict(
        gmix=norm_mix_g.reshape(1, -1), w_inT=cast_t(w_in), gq=col(q_norm_g), gk=col(k_norm_g),
        gsg=col(sg_norm_g), wsT=jnp.swapaxes(sg_w, 1, 2).astype(MXU_DTYPE),
        bs=sg_b.reshape(SG_GROUPS, 1, SG_CHUNK), wsgoT=cast_t(w_sg_o),
        waoT=cast_t(w_attn_o), woutT=cast_t(w_out), gxa=col(norm_xa_g), wqxT=cast_t(wq_xa),
        woxT=cast_t(wo_xa), gffn=col(norm_ffn_g), wpqT=cast_t(w_peer_q),
        k1=peer_k1.astype(MXU_DTYPE), k2=peer_k2.astype(MXU_DTYPE),
        e_down=expert_down.astype(MXU_DTYPE), e_upT=e_upT, gfin=col(final_norm_g),
        gmem=norm_mem_g.reshape(1, -1), wk=wkv_xa[:, :D_MODEL].astype(MXU_DTYPE), wvT=cast_t(wkv_xa[:, D_MODEL:]),
    )
```

```python
import functools
import math

import jax
import jax.numpy as jnp
from jax import lax
from jax.experimental import pallas as pl
from jax.experimental.pallas import tpu as pltpu

D_MODEL = 1024
GRID_W = 64
N_HEADS = 16
N_KV_HEADS = 4
HEAD_DIM = 64
KV_GROUP = N_HEADS // N_KV_HEADS
ATTN_Q_WIDTH = N_HEADS * HEAD_DIM
ATTN_KV_WIDTH = N_KV_HEADS * HEAD_DIM
ROPE_THETA = 10000.0
SG_WIDTH = 1024
SG_GROUPS = 8
SG_GROUP_DIM = SG_WIDTH // SG_GROUPS
SG_CHUNK = 128
XA_HEADS = 4
XA_HEAD_DIM = D_MODEL // XA_HEADS
PEER_HEADS = 8
PEER_N_KEYS = 128
PEER_D_KEY = 256
PEER_HALF = PEER_D_KEY // 2
PEER_TOPK = 16
PEER_CHUNK = 1024
EPS = 1e-6

_Q0 = 0
_K0 = _Q0 + ATTN_Q_WIDTH
_V0 = _K0 + ATTN_KV_WIDTH
_Z0 = _V0 + ATTN_KV_WIDTH
_G0 = _Z0 + 2 * SG_WIDTH
_IN_WIDTH = _G0 + 2 * D_MODEL

LANES = 128
SUBLANES = 8
V7X_VMEM_BYTES = 64 * 1024 * 1024

MXU_DTYPE = jnp.bfloat16
GATE_DTYPE = jnp.bfloat16
GATE_ROWS = 2 * SUBLANES
NEG_INF = float("-inf")


def _dot(a, b):
    return jnp.dot(a, b, preferred_element_type=jnp.float32)


def _dot_nt(a, b):
    return lax.dot_general(a, b, (((1,), (1,)), ((), ())), preferred_element_type=jnp.float32)


def _gelu(x):
    c = math.sqrt(2.0 / math.pi)
    return 0.5 * x * (1.0 + jnp.tanh(c * (x + 0.044715 * (x * x * x))))


def _sigmoid(x):
    return 1.0 / (1.0 + jnp.exp(-x))


def _rmsnorm_rows(x, g_row):
    ms = jnp.mean(x * x, axis=-1, keepdims=True)
    return x * lax.rsqrt(ms + EPS) * g_row


def _rmsnorm_cols(xT, g_col):
    ms = jnp.mean(xT * xT, axis=0, keepdims=True)
    return xT * lax.rsqrt(ms + EPS) * g_col


def _params(semantics, vmem_mb, flags=None):
    return pltpu.CompilerParams(dimension_semantics=semantics, vmem_limit_bytes=vmem_mb * 1024 * 1024,
                                flags=flags)


def _const_spec(shape):
    nd = len(shape)
    return pl.BlockSpec(shape, lambda *_: (0,) * nd)


def _kv_mem_kernel(mem_ref, g_ref, wk_ref, wvT_ref, k_ref, vT_ref):
    mn = _rmsnorm_rows(mem_ref[0], g_ref[...]).astype(MXU_DTYPE)
    k_ref[0] = _dot(mn, wk_ref[...]).astype(k_ref.dtype)
    vT_ref[0] = _dot_nt(wvT_ref[...], mn).astype(vT_ref.dtype)


def _kv_mem(mem, g_row, wk, wvT):
    nb, m, d = mem.shape
    return pl.pallas_call(
        _kv_mem_kernel,
        out_shape=(jax.ShapeDtypeStruct((nb, m, d), MXU_DTYPE), jax.ShapeDtypeStruct((nb, d, m), MXU_DTYPE)),
        grid=(nb,),
        in_specs=[pl.BlockSpec((1, m, d), lambda b: (b, 0, 0)), _const_spec((1, d)),
                  _const_spec((d, d)), _const_spec((d, d))],
        out_specs=(pl.BlockSpec((1, m, d), lambda b: (b, 0, 0)), pl.BlockSpec((1, d, m), lambda b: (b, 0, 0))),
        compiler_params=_params(("parallel",), 32),
        name="kv_mem",
    )(mem, g_row, wk, wvT)


def _head_norm_rope(t, g_col, cos, sin):
    ms = jnp.mean(t * t, axis=0, keepdims=True)
    t = t * lax.rsqrt(ms + EPS) * g_col
    q4 = HEAD_DIM // 4
    sw = jnp.concatenate([t[q4:2 * q4], t[0:q4], t[3 * q4:], t[2 * q4:3 * q4]], axis=0)
    return t * cos + sw * sin


def _inproj_kernel(x_ref, gmix_ref, w_ref, gq_ref, gk_ref, cos_ref, sin_ref, gsg_ref, wsT_ref, bs_ref, wsgo_ref,
                   qT_ref, k_ref, vT_ref, gattn_ref, sgp_ref):
    tm = x_ref.shape[1]
    h = _rmsnorm_rows(x_ref[0], gmix_ref[...]).astype(MXU_DTYPE)

    def proj_t(lo, hi):
        return _dot_nt(w_ref[lo:hi, :], h)

    cos = cos_ref[...]
    sin = sin_ref[...]

    q_t = proj_t(_Q0, _K0)
    scale = math.log2(math.e) / math.sqrt(HEAD_DIM)
    for hd in range(N_HEADS):
        r = _head_norm_rope(q_t[hd * HEAD_DIM:(hd + 1) * HEAD_DIM], gq_ref[...], cos, sin) * scale
        qT_ref[0, hd * HEAD_DIM:(hd + 1) * HEAD_DIM, :] = r.astype(qT_ref.dtype)

    k_t = proj_t(_K0, _V0)
    k_rot = jnp.concatenate(
        [_head_norm_rope(k_t[g * HEAD_DIM:(g + 1) * HEAD_DIM], gk_ref[...], cos, sin) for g in range(N_KV_HEADS)],
        axis=0)
    k_tok = k_rot.T
    for g in range(N_KV_HEADS):
        k_ref[0, g] = k_tok[:, g * HEAD_DIM:(g + 1) * HEAD_DIM].astype(k_ref.dtype)

    vT_ref[0] = proj_t(_V0, _Z0).astype(vT_ref.dtype)

    z_t = _gelu(proj_t(_Z0, _G0))
    u = z_t[:SG_WIDTH]
    vn = _rmsnorm_cols(z_t[SG_WIDTH:], gsg_ref[...]).astype(MXU_DTYPE)
    n_chunks = tm // SG_CHUNK
    sv_groups = []
    for g in range(SG_GROUPS):
        vg = vn[g * SG_GROUP_DIM:(g + 1) * SG_GROUP_DIM]
        lhs = jnp.concatenate([vg[:, c * SG_CHUNK:(c + 1) * SG_CHUNK] for c in range(n_chunks)], axis=0)
        r = _dot(lhs, wsT_ref[g]) + bs_ref[g]
        sv_groups.append(jnp.concatenate(
            [r[c * SG_GROUP_DIM:(c + 1) * SG_GROUP_DIM] for c in range(n_chunks)], axis=1))
    sg = (u * jnp.concatenate(sv_groups, axis=0)).astype(MXU_DTYPE)
    sg_branch = _dot(wsgo_ref[...], sg)

    gates = _sigmoid(proj_t(_G0, _IN_WIDTH))
    gattn_ref[0] = gates[:D_MODEL].astype(gattn_ref.dtype)
    sgp_ref[0] = (gates[D_MODEL:] * sg_branch).astype(sgp_ref.dtype)


def _inproj(x, gmix, w_inT, gq, gk, cos_t, sin_t, gsg, wsT, bs, wsgoT, tm):
    b, s, d = x.shape
    grid = (b, s // tm)
    tok = lambda rows: pl.BlockSpec((1, rows, tm), lambda i, j: (i, 0, j))
    return pl.pallas_call(
        _inproj_kernel,
        out_shape=(jax.ShapeDtypeStruct((b, ATTN_Q_WIDTH, s), MXU_DTYPE),
                   jax.ShapeDtypeStruct((b, N_KV_HEADS, s, HEAD_DIM), MXU_DTYPE),
                   jax.ShapeDtypeStruct((b, ATTN_KV_WIDTH, s), MXU_DTYPE),
                   jax.ShapeDtypeStruct((b, D_MODEL, s), MXU_DTYPE),
                   jax.ShapeDtypeStruct((b, D_MODEL, s), jnp.float32)),
        grid=grid,
        in_specs=[pl.BlockSpec((1, tm, d), lambda i, j: (i, j, 0)),
                  _const_spec((1, d)), _const_spec(w_inT.shape),
                  _const_spec((HEAD_DIM, 1)), _const_spec((HEAD_DIM, 1)),
                  pl.BlockSpec((HEAD_DIM, tm), lambda i, j: (0, j)),
                  pl.BlockSpec((HEAD_DIM, tm), lambda i, j: (0, j)),
                  _const_spec((SG_WIDTH, 1)), _const_spec(wsT.shape), _const_spec(bs.shape),
                  _const_spec(wsgoT.shape)],
        out_specs=(tok(ATTN_Q_WIDTH),
                   pl.BlockSpec((1, N_KV_HEADS, tm, HEAD_DIM), lambda i, j: (i, 0, j, 0)),
                   tok(ATTN_KV_WIDTH), tok(D_MODEL), tok(D_MODEL)),
        compiler_params=_params(("parallel", "parallel"), 56),
        name="inproj",
    )(x, gmix, w_inT, gq, gk, cos_t, sin_t, gsg, wsT, bs, wsgoT)


def _attn_kernel(qT_ref, k_ref, vT_ref, o_ref):
    tq = qT_ref.shape[2]
    k = k_ref[0, 0]
    vT = vT_ref[0]
    vT1 = jnp.concatenate([vT, jnp.ones((GATE_ROWS, vT.shape[1]), vT.dtype)], axis=0)
    for pair in range(KV_GROUP // 2):
        heads = (2 * pair, 2 * pair + 1)
        qT2 = jnp.concatenate([qT_ref[0, a * HEAD_DIM:(a + 1) * HEAD_DIM, :] for a in heads], axis=1)
        sT = _dot(k, qT2).astype(MXU_DTYPE)
        m = jnp.max(sT, axis=0, keepdims=True)
        p = jnp.exp2(sT - m)
        o = _dot(vT1, p)
        oT = o[:HEAD_DIM] / o[HEAD_DIM:HEAD_DIM + 1]
        for i, a in enumerate(heads):
            o_ref[0, a * HEAD_DIM:(a + 1) * HEAD_DIM, :] = oT[:, i * tq:(i + 1) * tq].astype(o_ref.dtype)


def _attention(qT, k, vT, tq):
    b, _, s = qT.shape
    rows = KV_GROUP * HEAD_DIM
    return pl.pallas_call(
        _attn_kernel,
        out_shape=jax.ShapeDtypeStruct((b, ATTN_Q_WIDTH, s), MXU_DTYPE),
        grid=(b, N_KV_HEADS, s // tq),
        in_specs=[pl.BlockSpec((1, rows, tq), lambda i, g, j: (i, g, j)),
                  pl.BlockSpec((1, 1, s, HEAD_DIM), lambda i, g, j: (i, g, 0, 0)),
                  pl.BlockSpec((1, HEAD_DIM, s), lambda i, g, j: (i, g, 0))],
        out_specs=pl.BlockSpec((1, rows, tq), lambda i, g, j: (i, g, j)),
        compiler_params=_params(("parallel", "parallel", "parallel"), 48),
        name="attn",
    )(qT, k, vT)


def _merge_kernel(x_ref, attn_ref, gattn_ref, sgp_ref, wao_ref, wout_ref, gxa_ref, wq_ref, km_ref, vmT_ref,
                  wo_ref, gffn_ref, x2_ref, hp_ref):
    xT = x_ref[0].T
    attn_branch = _dot(wao_ref[...], attn_ref[0])
    mix = gattn_ref[0].astype(jnp.float32) * attn_branch + sgp_ref[0]
    x1 = xT + _dot(wout_ref[...], mix.astype(MXU_DTYPE))

    hx = _rmsnorm_cols(x1, gxa_ref[...]).astype(MXU_DTYPE)
    qx = (_dot(wq_ref[...], hx) * (1.0 / math.sqrt(XA_HEAD_DIM))).astype(MXU_DTYPE)
    outs = []
    for hd in range(XA_HEADS):
        lo, hi = hd * XA_HEAD_DIM, (hd + 1) * XA_HEAD_DIM
        sT = _dot(km_ref[0, :, lo:hi], qx[lo:hi])
        m = jnp.max(sT, axis=0, keepdims=True)
        p = jnp.exp(sT - m)
        l = jnp.sum(p, axis=0, keepdims=True)
        outs.append((_dot(vmT_ref[0, lo:hi, :], p.astype(MXU_DTYPE)) / l).astype(MXU_DTYPE))
    x2 = x1 + _dot(wo_ref[...], jnp.concatenate(outs, axis=0))
    x2_ref[0] = x2
    hp_ref[0] = _rmsnorm_cols(x2, gffn_ref[...]).astype(hp_ref.dtype)


def _merge(x, attnT, gattnT, sgpT, waoT, woutT, gxa, wqT, kmem, vmemT, woT, gffn, tm):
    b, s, d = x.shape
    m = kmem.shape[1]
    tok = pl.BlockSpec((1, d, tm), lambda i, j: (i, 0, j))
    wspec = _const_spec((d, d))
    col = _const_spec((d, 1))
    return pl.pallas_call(
        _merge_kernel,
        out_shape=(jax.ShapeDtypeStruct((b, d, s), jnp.float32), jax.ShapeDtypeStruct((b, d, s), MXU_DTYPE)),
        grid=(b, s // tm),
        in_specs=[pl.BlockSpec((1, tm, d), lambda i, j: (i, j, 0)), tok, tok, tok, wspec, wspec, col, wspec,
                  pl.BlockSpec((1, m, d), lambda i, j: (i, 0, 0)), pl.BlockSpec((1, d, m), lambda i, j: (i, 0, 0)),
                  wspec, col],
        out_specs=(tok, tok),
        compiler_params=_params(("parallel", "parallel"), 48),
        name="merge",
    )(x, attnT, gattnT, sgpT, waoT, woutT, gxa, wqT, kmem, vmemT, woT, gffn)


def _top_values(s, top_ref):
    cur = s
    rank = jnp.full(s.shape, float(PEER_TOPK), jnp.float32)
    for r in range(PEER_TOPK):
        mx = jnp.max(cur, axis=0, keepdims=True)
        top_ref[r:r + 1, :] = mx
        hit = cur == mx
        rank = jnp.where(hit, float(r), rank)
        if r + 1 < PEER_TOPK:
            cur = jnp.where(hit, NEG_INF, cur)
    return rank


def _stair_candidates(v1, v2):
    half = PEER_TOPK // 2
    row8 = lax.broadcasted_iota(jnp.int32, (half, 1), 0)
    cands = [v1 + v2[0:1], v1[0:1] + v2[half:]]
    for bb in range(1, half):
        a_max = PEER_TOPK // (bb + 1) - 1
        c = v1[:half] + v2[bb:bb + 1]
        cands.append(jnp.where(row8 <= a_max, c, NEG_INF))
    return jnp.concatenate(cands, axis=0)


def _peer_sel_kernel(hp_ref, wq_ref, k1_ref, k2_ref, nb_ref, p1_ref, r2_ref, p2_ref, v1_s, v2_s):
    qp = _dot(wq_ref[...], hp_ref[0]).astype(MXU_DTYPE)
    for hd in range(PEER_HEADS):
        base = hd * PEER_D_KEY
        s1 = _dot(k1_ref[hd], qp[base:base + PEER_HALF])
        s2 = _dot(k2_ref[hd], qp[base + PEER_HALF:base + PEER_D_KEY])
        r1 = _top_values(s1, v1_s)
        r2 = _top_values(s2, v2_s)
        v1 = v1_s[...]
        v2 = v2_s[...]
        cand = _stair_candidates(v1, v2)
        cur = cand
        for r in range(PEER_TOPK):
            t = jnp.max(cur, axis=0, keepdims=True)
            if r + 1 < PEER_TOPK:
                cur = jnp.where(cur == t, NEG_INF, cur)
        m1 = v1[0:1]
        m2 = v2[0:1]
        z = jnp.sum(jnp.where(cand >= t, jnp.exp(cand - (m1 + m2)), 0.0), axis=0, keepdims=True)
        nb_top = jnp.zeros_like(v1)
        for b in range(PEER_TOPK):
            nb_top = nb_top + jnp.where(v1 + v2[b:b + 1] >= t, 1.0, 0.0)
        nb = jnp.zeros_like(s1)
        for a in range(PEER_TOPK):
            nb = jnp.where(r1 == float(a), nb_top[a:a + 1], nb)
        nb_ref[0, hd] = _pair_words(nb)
        p1_ref[0, hd] = _pair_words(jnp.exp(s1 - m1) / z)
        r2_ref[0, hd] = r2.astype(r2_ref.dtype)
        p2_ref[0, hd] = jnp.exp(s2 - m2).astype(p2_ref.dtype)


def _pair_words(x):
    u = lax.bitcast_convert_type(x.astype(GATE_DTYPE).astype(jnp.float32), jnp.uint32)
    return u | (u >> 16)


def _peer_sel(hpT, wpqT, k1, k2, tm):
    b, d, s = hpT.shape
    shape = (b, PEER_HEADS, PEER_N_KEYS, s)
    words = jax.ShapeDtypeStruct(shape, jnp.uint32)
    sel = jax.ShapeDtypeStruct(shape, GATE_DTYPE)
    sel_spec = pl.BlockSpec((1, PEER_HEADS, PEER_N_KEYS, tm), lambda i, j: (i, 0, 0, j))
    return pl.pallas_call(
        _peer_sel_kernel,
        out_shape=(words, words, sel, sel),
        grid=(b, s // tm),
        in_specs=[pl.BlockSpec((1, d, tm), lambda i, j: (i, 0, j)), _const_spec(wpqT.shape),
                  _const_spec(k1.shape), _const_spec(k2.shape)],
        out_specs=(sel_spec, sel_spec, sel_spec, sel_spec),
        scratch_shapes=[pltpu.VMEM((PEER_TOPK, tm), jnp.float32), pltpu.VMEM((PEER_TOPK, tm), jnp.float32)],
        compiler_params=_params(("parallel", "parallel"), 48),
        name="peer_sel",
    )(hpT, wpqT, k1, k2)


def _peer_mix_kernel(hp_ref, ed_ref, euT_ref, nb_ref, p1_ref, r2_ref, p2_ref, x2_ref, gfin_ref, y_ref,
                     acc_ref, act_ref, w_ref, row_s, key_s):
    c = pl.program_id(2)
    n_c = pl.num_programs(2)
    tt = hp_ref.shape[2]
    n_i = ed_ref.shape[0] // PEER_N_KEYS
    jb = GATE_ROWS
    n_k = PEER_N_KEYS // jb

    @pl.when(c == 0)
    def _():
        acc_ref[...] = jnp.zeros_like(acc_ref)
        key_s[0] = r2_ref[0]
        key_s[1] = p2_ref[0]

    row_s[0] = nb_ref[0]
    row_s[1] = p1_ref[0]

    act_ref[...] = _gelu(_dot(ed_ref[...], hp_ref[0])).astype(act_ref.dtype)

    def row_bcast(words):
        return pltpu.bitcast(jnp.broadcast_to(words, (SUBLANES, LANES)), GATE_DTYPE)

    zero = jnp.zeros((jb, LANES), GATE_DTYPE)

    for lb in range(tt // LANES):
        ls = pl.ds(lb * LANES, LANES)
        for il in range(n_i):
            accs = [None] * n_k
            for hd in range(PEER_HEADS):
                bn = row_bcast(row_s[0, hd, il:il + 1, ls])
                bp = row_bcast(row_s[1, hd, il:il + 1, ls])
                for k in range(n_k):
                    r2v = key_s[0, hd, k * jb:(k + 1) * jb, ls]
                    p2v = key_s[1, hd, k * jb:(k + 1) * jb, ls]
                    term = bp * jnp.where(r2v < bn, p2v, zero)
                    accs[k] = term if accs[k] is None else accs[k] + term
            for k in range(n_k):
                rows = pl.ds(il * PEER_N_KEYS + k * jb, jb)
                w_ref[rows, ls] = (accs[k] * act_ref[rows, ls]).astype(w_ref.dtype)

    acc_ref[...] += _dot(euT_ref[0], w_ref[...])

    @pl.when(c == n_c - 1)
    def _():
        x3 = x2_ref[0] + acc_ref[...]
        y_ref[0] = _rmsnorm_cols(x3, gfin_ref[...]).T.astype(y_ref.dtype)


def _peer_mix(hpT, e_down, e_upT, nb, p1, r2, p2, x2T, gfin, tt):
    b, d, s = hpT.shape
    n_e = e_down.shape[0]
    ec = PEER_CHUNK
    n_i = ec // PEER_N_KEYS
    tok = lambda rows: pl.BlockSpec((1, rows, tt), lambda i, j, c: (i, 0, j))
    row_spec = pl.BlockSpec((1, PEER_HEADS, n_i, tt), lambda i, j, c: (i, 0, c, j))
    full_spec = pl.BlockSpec((1, PEER_HEADS, PEER_N_KEYS, tt), lambda i, j, c: (i, 0, 0, j))
    return pl.pallas_call(
        _peer_mix_kernel,
        out_shape=jax.ShapeDtypeStruct((b, s, d), jnp.float32),
        grid=(b, s // tt, n_e // ec),
        in_specs=[tok(d),
                  pl.BlockSpec((ec, d), lambda i, j, c: (c, 0)),
                  pl.BlockSpec((1, d, ec), lambda i, j, c: (c, 0, 0)),
                  row_spec, row_spec, full_spec, full_spec, tok(d), _const_spec((d, 1))],
        out_specs=pl.BlockSpec((1, tt, d), lambda i, j, c: (i, j, 0)),
        scratch_shapes=[pltpu.VMEM((d, tt), jnp.float32), pltpu.VMEM((ec, tt), GATE_DTYPE),
                        pltpu.VMEM((ec, tt), MXU_DTYPE),
                        pltpu.VMEM((2, PEER_HEADS, n_i, tt), jnp.uint32),
                        pltpu.VMEM((2, PEER_HEADS, PEER_N_KEYS, tt), GATE_DTYPE)],
        compiler_params=_params(("parallel", "parallel", "arbitrary"), 56),
        name="peer_mix",
    )(hpT, e_down, e_upT, nb, p1, r2, p2, x2T, gfin)


def _rope_tables(s):
    rows = s // GRID_W
    row_pos = jnp.repeat(jnp.arange(rows, dtype=jnp.float32), GRID_W)
    col_pos = jnp.tile(jnp.arange(GRID_W, dtype=jnp.float32), rows)
    n_freq = HEAD_DIM // 4
    inv_freq = ROPE_THETA ** (-jnp.arange(n_freq, dtype=jnp.float32) / n_freq)
    ang_r = inv_freq[:, None] * row_pos[None, :]
    ang_c = inv_freq[:, None] * col_pos[None, :]
    cos_t = jnp.concatenate([jnp.cos(ang_r), jnp.cos(ang_r), jnp.cos(ang_c), jnp.cos(ang_c)], axis=0)
    sin_t = jnp.concatenate([-jnp.sin(ang_r), jnp.sin(ang_r), -jnp.sin(ang_c), jnp.sin(ang_c)], axis=0)
    return cos_t, sin_t


def _tiles(s):
    return dict(inproj=min(512, s), attn=256, merge=min(512, s), sel=256, mix=min(512, s))


def _layer(x, kmem, vmemT, w):
    s = x.shape[1]
    t = _tiles(s)
    cos_t, sin_t = _rope_tables(s)
    qT, k, vT, gattnT, sgpT = _inproj(x, w["gmix"], w["w_inT"], w["gq"], w["gk"], cos_t, sin_t, w["gsg"],
                                      w["wsT"], w["bs"], w["wsgoT"], t["inproj"])
    attnT = _attention(qT, k, vT, t["attn"])
    x2T, hpT = _merge(x, attnT, gattnT, sgpT, w["waoT"], w["woutT"], w["gxa"], w["wqxT"], kmem, vmemT,
                      w["woxT"], w["gffn"], t["merge"])
    nb, p1, r2, p2 = _peer_sel(hpT, w["wpqT"], w["k1"], w["k2"], t["sel"])
    return _peer_mix(hpT, w["e_down"], w["e_upT"], nb, p1, r2, p2, x2T, w["gfin"], t["mix"])


def kernel(x_prompt, x_sample, mem_prompt, mem_sample, norm_mix_g, w_in, q_norm_g, k_norm_g, sg_norm_g, sg_w,
           sg_b, w_attn_o, w_sg_o, w_out, norm_xa_g, norm_mem_g, wq_xa, wkv_xa, wo_xa, norm_ffn_g, w_peer_q,
           peer_k1, peer_k2, expert_down, expert_up, final_norm_g):
    assert w_in.shape[0] == 1, "single-layer trunk"
    w = _prep_weights(norm_mix_g[0], w_in[0], q_norm_g[0], k_norm_g[0], sg_norm_g[0], sg_w[0], sg_b[0],
                      w_attn_o[0], w_sg_o[0], w_out[0], norm_xa_g[0], norm_mem_g[0], wq_xa[0], wkv_xa[0],
                      wo_xa[0], norm_ffn_g[0], w_peer_q[0], peer_k1[0], peer_k2[0], expert_down[0],
                      expert_up[0], final_norm_g)
    outs = []
    for x, mem in ((x_prompt, mem_prompt), (x_sample, mem_sample)):
        kmem, vmemT = _kv_mem(mem, w["gmem"], w["wk"], w["wvT"])
        outs.append(_layer(x, kmem, vmemT, w))
    return tuple(outs)


def _prep_weights(norm_mix_g, w_in, q_norm_g, k_norm_g, sg_norm_g, sg_w, sg_b, w_attn_o, w_sg_o, w_out,
                  norm_xa_g, norm_mem_g, wq_xa, wkv_xa, wo_xa, norm_ffn_g, w_peer_q, peer_k1, peer_k2,
                  expert_down, expert_up, final_norm_g):
    cast_t = lambda a: a.T.astype(MXU_DTYPE)
    col = lambda g: g.reshape(-1, 1).astype(jnp.float32)
    n_e, d = expert_up.shape
    e_upT = jnp.swapaxes(expert_up.reshape(n_e // PEER_CHUNK, PEER_CHUNK, d), 1, 2).astype(MXU_DTYPE)
    return dict(
        gmix=norm_mix_g.reshape(1, -1), w_inT=cast_t(w_in), gq=col(q_norm_g), gk=col(k_norm_g),
        gsg=col(sg_norm_g), wsT=jnp.swapaxes(sg_w, 1, 2).astype(MXU_DTYPE),
        bs=sg_b.reshape(SG_GROUPS, 1, SG_CHUNK), wsgoT=cast_t(w_sg_o),
        waoT=cast_t(w_attn_o), woutT=cast_t(w_out), gxa=col(norm_xa_g), wqxT=cast_t(wq_xa),
        woxT=cast_t(wo_xa), gffn=col(norm_ffn_g), wpqT=cast_t(w_peer_q),
        k1=peer_k1.astype(MXU_DTYPE), k2=peer_k2.astype(MXU_DTYPE),
        e_down=expert_down.astype(MXU_DTYPE), e_upT=e_upT, gfin=col(final_norm_g),
        gmem=norm_mem_g.reshape(1, -1), wk=wkv_xa[:, :D_MODEL].astype(MXU_DTYPE), wvT=cast_t(wkv_xa[:, D_MODEL:]),
    )
```

```python
import functools
import math

import jax
import jax.numpy as jnp
from jax import lax
from jax.experimental import pallas as pl
from jax.experimental.pallas import tpu as pltpu

D_MODEL = 1024
GRID_W = 64
N_HEADS = 16
N_KV_HEADS = 4
HEAD_DIM = 64
KV_GROUP = N_HEADS // N_KV_HEADS
ATTN_Q_WIDTH = N_HEADS * HEAD_DIM
ATTN_KV_WIDTH = N_KV_HEADS * HEAD_DIM
ROPE_THETA = 10000.0
SG_WIDTH = 1024
SG_GROUPS = 8
SG_GROUP_DIM = SG_WIDTH // SG_GROUPS
SG_CHUNK = 128
XA_HEADS = 4
XA_HEAD_DIM = D_MODEL // XA_HEADS
PEER_HEADS = 8
PEER_N_KEYS = 128
PEER_D_KEY = 256
PEER_HALF = PEER_D_KEY // 2
PEER_TOPK = 16
PEER_CHUNK = 1024
EPS = 1e-6

_Q0 = 0
_K0 = _Q0 + ATTN_Q_WIDTH
_V0 = _K0 + ATTN_KV_WIDTH
_Z0 = _V0 + ATTN_KV_WIDTH
_G0 = _Z0 + 2 * SG_WIDTH
_IN_WIDTH = _G0 + 2 * D_MODEL

LANES = 128
SUBLANES = 8
V7X_VMEM_BYTES = 64 * 1024 * 1024

MXU_DTYPE = jnp.bfloat16
GATE_DTYPE = jnp.bfloat16
GATE_ROWS = 2 * SUBLANES
NEG_INF = float("-inf")


def _dot(a, b):
    return jnp.dot(a, b, preferred_element_type=jnp.float32)


def _dot_nt(a, b):
    return lax.dot_general(a, b, (((1,), (1,)), ((), ())), preferred_element_type=jnp.float32)


def _gelu(x):
    c = math.sqrt(2.0 / math.pi)
    return 0.5 * x * (1.0 + jnp.tanh(c * (x + 0.044715 * (x * x * x))))


def _sigmoid(x):
    return 1.0 / (1.0 + jnp.exp(-x))


def _rmsnorm_rows(x, g_row):
    ms = jnp.mean(x * x, axis=-1, keepdims=True)
    return x * lax.rsqrt(ms + EPS) * g_row


def _rmsnorm_cols(xT, g_col):
    ms = jnp.mean(xT * xT, axis=0, keepdims=True)
    return xT * lax.rsqrt(ms + EPS) * g_col


def _params(semantics, vmem_mb, flags=None):
    return pltpu.CompilerParams(dimension_semantics=semantics, vmem_limit_bytes=vmem_mb * 1024 * 1024,
                                flags=flags)


def _const_spec(shape):
    nd = len(shape)
    return pl.BlockSpec(shape, lambda *_: (0,) * nd)


def _kv_mem_kernel(mem_ref, g_ref, wk_ref, wvT_ref, k_ref, vT_ref):
    mn = _rmsnorm_rows(mem_ref[0], g_ref[...]).astype(MXU_DTYPE)
    k_ref[0] = _dot(mn, wk_ref[...]).astype(k_ref.dtype)
    vT_ref[0] = _dot_nt(wvT_ref[...], mn).astype(vT_ref.dtype)


def _kv_mem(mem, g_row, wk, wvT):
    nb, m, d = mem.shape
    return pl.pallas_call(
        _kv_mem_kernel,
        out_shape=(jax.ShapeDtypeStruct((nb, m, d), MXU_DTYPE), jax.ShapeDtypeStruct((nb, d, m), MXU_DTYPE)),
        grid=(nb,),
        in_specs=[pl.BlockSpec((1, m, d), lambda b: (b, 0, 0)), _const_spec((1, d)),
                  _const_spec((d, d)), _const_spec((d, d))],
        out_specs=(pl.BlockSpec((1, m, d), lambda b: (b, 0, 0)), pl.BlockSpec((1, d, m), lambda b: (b, 0, 0))),
        compiler_params=_params(("parallel",), 32),
        name="kv_mem",
    )(mem, g_row, wk, wvT)


def _head_norm_rope(t, g_col, cos, sin):
    ms = jnp.mean(t * t, axis=0, keepdims=True)
    t = t * lax.rsqrt(ms + EPS) * g_col
    q4 = HEAD_DIM // 4
    sw = jnp.concatenate([t[q4:2 * q4], t[0:q4], t[3 * q4:], t[2 * q4:3 * q4]], axis=0)
    return t * cos + sw * sin


def _inproj_kernel(x_ref, gmix_ref, w_ref, gq_ref, gk_ref, cos_ref, sin_ref, gsg_ref, wsT_ref, bs_ref, wsgo_ref,
                   qT_ref, k_ref, vT_ref, gattn_ref, sgp_ref):
    tm = x_ref.shape[1]
    h = _rmsnorm_rows(x_ref[0], gmix_ref[...]).astype(MXU_DTYPE)

    def proj_t(lo, hi):
        return _dot_nt(w_ref[lo:hi, :], h)

    cos = cos_ref[...]
    sin = sin_ref[...]

    q_t = proj_t(_Q0, _K0)
    scale = math.log2(math.e) / math.sqrt(HEAD_DIM)
    for hd in range(N_HEADS):
        r = _head_norm_rope(q_t[hd * HEAD_DIM:(hd + 1) * HEAD_DIM], gq_ref[...], cos, sin) * scale
        qT_ref[0, hd * HEAD_DIM:(hd + 1) * HEAD_DIM, :] = r.astype(qT_ref.dtype)

    k_t = proj_t(_K0, _V0)
    k_rot = jnp.concatenate(
        [_head_norm_rope(k_t[g * HEAD_DIM:(g + 1) * HEAD_DIM], gk_ref[...], cos, sin) for g in range(N_KV_HEADS)],
        axis=0)
    k_tok = k_rot.T
    for g in range(N_KV_HEADS):
        k_ref[0, g] = k_tok[:, g * HEAD_DIM:(g + 1) * HEAD_DIM].astype(k_ref.dtype)

    vT_ref[0] = proj_t(_V0, _Z0).astype(vT_ref.dtype)

    z_t = _gelu(proj_t(_Z0, _G0))
    u = z_t[:SG_WIDTH]
    vn = _rmsnorm_cols(z_t[SG_WIDTH:], gsg_ref[...]).astype(MXU_DTYPE)
    n_chunks = tm // SG_CHUNK
    sv_groups = []
    for g in range(SG_GROUPS):
        vg = vn[g * SG_GROUP_DIM:(g + 1) * SG_GROUP_DIM]
        lhs = jnp.concatenate([vg[:, c * SG_CHUNK:(c + 1) * SG_CHUNK] for c in range(n_chunks)], axis=0)
        r = _dot(lhs, wsT_ref[g]) + bs_ref[g]
        sv_groups.append(jnp.concatenate(
            [r[c * SG_GROUP_DIM:(c + 1) * SG_GROUP_DIM] for c in range(n_chunks)], axis=1))
    sg = (u * jnp.concatenate(sv_groups, axis=0)).astype(MXU_DTYPE)
    sg_branch = _dot(wsgo_ref[...], sg)

    gates = _sigmoid(proj_t(_G0, _IN_WIDTH))
    gattn_ref[0] = gates[:D_MODEL].astype(gattn_ref.dtype)
    sgp_ref[0] = (gates[D_MODEL:] * sg_branch).astype(sgp_ref.dtype)


def _inproj(x, gmix, w_inT, gq, gk, cos_t, sin_t, gsg, wsT, bs, wsgoT, tm):
    b, s, d = x.shape
    grid = (b, s // tm)
    tok = lambda rows: pl.BlockSpec((1, rows, tm), lambda i, j: (i, 0, j))
    return pl.pallas_call(
        _inproj_kernel,
        out_shape=(jax.ShapeDtypeStruct((b, ATTN_Q_WIDTH, s), MXU_DTYPE),
                   jax.ShapeDtypeStruct((b, N_KV_HEADS, s, HEAD_DIM), MXU_DTYPE),
                   jax.ShapeDtypeStruct((b, ATTN_KV_WIDTH, s), MXU_DTYPE),
                   jax.ShapeDtypeStruct((b, D_MODEL, s), MXU_DTYPE),
                   jax.ShapeDtypeStruct((b, D_MODEL, s), jnp.float32)),
        grid=grid,
        in_specs=[pl.BlockSpec((1, tm, d), lambda i, j: (i, j, 0)),
                  _const_spec((1, d)), _const_spec(w_inT.shape),
                  _const_spec((HEAD_DIM, 1)), _const_spec((HEAD_DIM, 1)),
                  pl.BlockSpec((HEAD_DIM, tm), lambda i, j: (0, j)),
                  pl.BlockSpec((HEAD_DIM, tm), lambda i, j: (0, j)),
                  _const_spec((SG_WIDTH, 1)), _const_spec(wsT.shape), _const_spec(bs.shape),
                  _const_spec(wsgoT.shape)],
        out_specs=(tok(ATTN_Q_WIDTH),
                   pl.BlockSpec((1, N_KV_HEADS, tm, HEAD_DIM), lambda i, j: (i, 0, j, 0)),
                   tok(ATTN_KV_WIDTH), tok(D_MODEL), tok(D_MODEL)),
        compiler_params=_params(("parallel", "parallel"), 56),
        name="inproj",
    )(x, gmix, w_inT, gq, gk, cos_t, sin_t, gsg, wsT, bs, wsgoT)


def _attn_kernel(qT_ref, k_ref, vT_ref, o_ref):
    tq = qT_ref.shape[2]
    k = k_ref[0, 0]
    vT = vT_ref[0]
    vT1 = jnp.concatenate([vT, jnp.ones((GATE_ROWS, vT.shape[1]), vT.dtype)], axis=0)
    for pair in range(KV_GROUP // 2):
        heads = (2 * pair, 2 * pair + 1)
        qT2 = jnp.concatenate([qT_ref[0, a * HEAD_DIM:(a + 1) * HEAD_DIM, :] for a in heads], axis=1)
        sT = _dot(k, qT2).astype(MXU_DTYPE)
        m = jnp.max(sT, axis=0, keepdims=True)
        p = jnp.exp2(sT - m)
        o = _dot(vT1, p)
        oT = o[:HEAD_DIM] / o[HEAD_DIM:HEAD_DIM + 1]
        for i, a in enumerate(heads):
            o_ref[0, a * HEAD_DIM:(a + 1) * HEAD_DIM, :] = oT[:, i * tq:(i + 1) * tq].astype(o_ref.dtype)


def _attention(qT, k, vT, tq):
    b, _, s = qT.shape
    rows = KV_GROUP * HEAD_DIM
    return pl.pallas_call(
        _attn_kernel,
        out_shape=jax.ShapeDtypeStruct((b, ATTN_Q_WIDTH, s), MXU_DTYPE),
        grid=(b, N_KV_HEADS, s // tq),
        in_specs=[pl.BlockSpec((1, rows, tq), lambda i, g, j: (i, g, j)),
                  pl.BlockSpec((1, 1, s, HEAD_DIM), lambda i, g, j: (i, g, 0, 0)),
                  pl.BlockSpec((1, HEAD_DIM, s), lambda i, g, j: (i, g, 0))],
        out_specs=pl.BlockSpec((1, rows, tq), lambda i, g, j: (i, g, j)),
        compiler_params=_params(("parallel", "parallel", "parallel"), 48),
        name="attn",
    )(qT, k, vT)


def _merge_kernel(x_ref, attn_ref, gattn_ref, sgp_ref, wao_ref, wout_ref, gxa_ref, wq_ref, km_ref, vmT_ref,
                  wo_ref, gffn_ref, x2_ref, hp_ref):
    xT = x_ref[0].T
    attn_branch = _dot(wao_ref[...], attn_ref[0])
    mix = gattn_ref[0].astype(jnp.float32) * attn_branch + sgp_ref[0]
    x1 = xT + _dot(wout_ref[...], mix.astype(MXU_DTYPE))

    hx = _rmsnorm_cols(x1, gxa_ref[...]).astype(MXU_DTYPE)
    qx = (_dot(wq_ref[...], hx) * (1.0 / math.sqrt(XA_HEAD_DIM))).astype(MXU_DTYPE)
    outs = []
    for hd in range(XA_HEADS):
        lo, hi = hd * XA_HEAD_DIM, (hd + 1) * XA_HEAD_DIM
        sT = _dot(km_ref[0, :, lo:hi], qx[lo:hi])
        m = jnp.max(sT, axis=0, keepdims=True)
        p = jnp.exp(sT - m)
        l = jnp.sum(p, axis=0, keepdims=True)
        outs.append((_dot(vmT_ref[0, lo:hi, :], p.astype(MXU_DTYPE)) / l).astype(MXU_DTYPE))
    x2 = x1 + _dot(wo_ref[...], jnp.concatenate(outs, axis=0))
    x2_ref[0] = x2
    hp_ref[0] = _rmsnorm_cols(x2, gffn_ref[...]).astype(hp_ref.dtype)


def _merge(x, attnT, gattnT, sgpT, waoT, woutT, gxa, wqT, kmem, vmemT, woT, gffn, tm):
    b, s, d = x.shape
    m = kmem.shape[1]
    tok = pl.BlockSpec((1, d, tm), lambda i, j: (i, 0, j))
    wspec = _const_spec((d, d))
    col = _const_spec((d, 1))
    return pl.pallas_call(
        _merge_kernel,
        out_shape=(jax.ShapeDtypeStruct((b, d, s), jnp.float32), jax.ShapeDtypeStruct((b, d, s), MXU_DTYPE)),
        grid=(b, s // tm),
        in_specs=[pl.BlockSpec((1, tm, d), lambda i, j: (i, j, 0)), tok, tok, tok, wspec, wspec, col, wspec,
                  pl.BlockSpec((1, m, d), lambda i, j: (i, 0, 0)), pl.BlockSpec((1, d, m), lambda i, j: (i, 0, 0)),
                  wspec, col],
        out_specs=(tok, tok),
        compiler_params=_params(("parallel", "parallel"), 48),
        name="merge",
    )(x, attnT, gattnT, sgpT, waoT, woutT, gxa, wqT, kmem, vmemT, woT, gffn)


def _top_values(s, top_ref):
    cur = s
    rank = jnp.full(s.shape, float(PEER_TOPK), jnp.float32)
    for r in range(PEER_TOPK):
        mx = jnp.max(cur, axis=0, keepdims=True)
        top_ref[r:r + 1, :] = mx
        hit = cur == mx
        rank = jnp.where(hit, float(r), rank)
        if r + 1 < PEER_TOPK:
            cur = jnp.where(hit, NEG_INF, cur)
    return rank


def _stair_candidates(v1, v2):
    half = PEER_TOPK // 2
    row8 = lax.broadcasted_iota(jnp.int32, (half, 1), 0)
    cands = [v1 + v2[0:1], v1[0:1] + v2[half:]]
    for bb in range(1, half):
        a_max = PEER_TOPK // (bb + 1) - 1
        c = v1[:half] + v2[bb:bb + 1]
        cands.append(jnp.where(row8 <= a_max, c, NEG_INF))
    return jnp.concatenate(cands, axis=0)


def _peer_sel_kernel(hp_ref, wq_ref, k1_ref, k2_ref, nb_ref, p1_ref, r2_ref, p2_ref, v1_s, v2_s):
    qp = _dot(wq_ref[...], hp_ref[0]).astype(MXU_DTYPE)
    for hd in range(PEER_HEADS):
        base = hd * PEER_D_KEY
        s1 = _dot(k1_ref[hd], qp[base:base + PEER_HALF])
        s2 = _dot(k2_ref[hd], qp[base + PEER_HALF:base + PEER_D_KEY])
        r1 = _top_values(s1, v1_s)
        r2 = _top_values(s2, v2_s)
        v1 = v1_s[...]
        v2 = v2_s[...]
        cand = _stair_candidates(v1, v2)
        cur = cand
        for r in range(PEER_TOPK):
            t = jnp.max(cur, axis=0, keepdims=True)
            if r + 1 < PEER_TOPK:
                cur = jnp.where(cur == t, NEG_INF, cur)
        m1 = v1[0:1]
        m2 = v2[0:1]
        z = jnp.sum(jnp.where(cand >= t, jnp.exp(cand - (m1 + m2)), 0.0), axis=0, keepdims=True)
        nb_top = jnp.zeros_like(v1)
        for b in range(PEER_TOPK):
            nb_top = nb_top + jnp.where(v1 + v2[b:b + 1] >= t, 1.0, 0.0)
        nb = jnp.zeros_like(s1)
        for a in range(PEER_TOPK):
            nb = jnp.where(r1 == float(a), nb_top[a:a + 1], nb)
        nb_ref[0, hd] = _pair_words(nb)
        p1_ref[0, hd] = _pair_words(jnp.exp(s1 - m1) / z)
        r2_ref[0, hd] = r2.astype(r2_ref.dtype)
        p2_ref[0, hd] = jnp.exp(s2 - m2).astype(p2_ref.dtype)


def _pair_words(x):
    u = lax.bitcast_convert_type(x.astype(GATE_DTYPE).astype(jnp.float32), jnp.uint32)
    return u | (u >> 16)


def _peer_sel(hpT, wpqT, k1, k2, tm):
    b, d, s = hpT.shape
    shape = (b, PEER_HEADS, PEER_N_KEYS, s)
    words = jax.ShapeDtypeStruct(shape, jnp.uint32)
    sel = jax.ShapeDtypeStruct(shape, GATE_DTYPE)
    sel_spec = pl.BlockSpec((1, PEER_HEADS, PEER_N_KEYS, tm), lambda i, j: (i, 0, 0, j))
    return pl.pallas_call(
        _peer_sel_kernel,
        out_shape=(words, words, sel, sel),
        grid=(b, s // tm),
        in_specs=[pl.BlockSpec((1, d, tm), lambda i, j: (i, 0, j)), _const_spec(wpqT.shape),
                  _const_spec(k1.shape), _const_spec(k2.shape)],
        out_specs=(sel_spec, sel_spec, sel_spec, sel_spec),
        scratch_shapes=[pltpu.VMEM((PEER_TOPK, tm), jnp.float32), pltpu.VMEM((PEER_TOPK, tm), jnp.float32)],
        compiler_params=_params(("parallel", "parallel"), 48),
        name="peer_sel",
    )(hpT, wpqT, k1, k2)


def _peer_mix_kernel(hp_ref, ed_ref, euT_ref, nb_ref, p1_ref, r2_ref, p2_ref, x2_ref, gfin_ref, y_ref,
                     acc_ref, act_ref, w_ref, row_s, key_s):
    c = pl.program_id(2)
    n_c = pl.num_programs(2)
    tt = hp_ref.shape[2]
    n_i = ed_ref.shape[0] // PEER_N_KEYS
    jb = GATE_ROWS
    n_k = PEER_N_KEYS // jb

    @pl.when(c == 0)
    def _():
        acc_ref[...] = jnp.zeros_like(acc_ref)
        key_s[0] = r2_ref[0]
        key_s[1] = p2_ref[0]

    row_s[0] = nb_ref[0]
    row_s[1] = p1_ref[0]

    act_ref[...] = _gelu(_dot(ed_ref[...], hp_ref[0])).astype(act_ref.dtype)

    def row_bcast(words):
        return pltpu.bitcast(jnp.broadcast_to(words, (SUBLANES, LANES)), GATE_DTYPE)

    zero = jnp.zeros((jb, LANES), GATE_DTYPE)

    for lb in range(tt // LANES):
        ls = pl.ds(lb * LANES, LANES)
        for il in range(n_i):
            accs = [None] * n_k
            for hd in range(PEER_HEADS):
                bn = row_bcast(row_s[0, hd, il:il + 1, ls])
                bp = row_bcast(row_s[1, hd, il:il + 1, ls])
                for k in range(n_k):
                    r2v = key_s[0, hd, k * jb:(k + 1) * jb, ls]
                    p2v = key_s[1, hd, k * jb:(k + 1) * jb, ls]
                    term = bp * jnp.where(r2v < bn, p2v, zero)
                    accs[k] = term if accs[k] is None else accs[k] + term
            for k in range(n_k):
                rows = pl.ds(il * PEER_N_KEYS + k * jb, jb)
                w_ref[rows, ls] = (accs[k] * act_ref[rows, ls]).astype(w_ref.dtype)

    acc_ref[...] += _dot(euT_ref[0], w_ref[...])

    @pl.when(c == n_c - 1)
    def _():
        x3 = x2_ref[0] + acc_ref[...]
        y_ref[0] = _rmsnorm_cols(x3, gfin_ref[...]).T.astype(y_ref.dtype)


def _peer_mix(hpT, e_down, e_upT, nb, p1, r2, p2, x2T, gfin, tt):
    b, d, s = hpT.shape
    n_e = e_down.shape[0]
    ec = PEER_CHUNK
    n_i = ec // PEER_N_KEYS
    tok = lambda rows: pl.BlockSpec((1, rows, tt), lambda i, j, c: (i, 0, j))
    row_spec = pl.BlockSpec((1, PEER_HEADS, n_i, tt), lambda i, j, c: (i, 0, c, j))
    full_spec = pl.BlockSpec((1, PEER_HEADS, PEER_N_KEYS, tt), lambda i, j, c: (i, 0, 0, j))
    return pl.pallas_call(
        _peer_mix_kernel,
        out_shape=jax.ShapeDtypeStruct((b, s, d), jnp.float32),
        grid=(b, s // tt, n_e // ec),
        in_specs=[tok(d),
                  pl.BlockSpec((ec, d), lambda i, j, c: (c, 0)),
                  pl.BlockSpec((1, d, ec), lambda i, j, c: (c, 0, 0)),
                  row_spec, row_spec, full_spec, full_spec, tok(d), _const_spec((d, 1))],
        out_specs=pl.BlockSpec((1, tt, d), lambda i, j, c: (i, j, 0)),
        scratch_shapes=[pltpu.VMEM((d, tt), jnp.float32), pltpu.VMEM((ec, tt), GATE_DTYPE),
                        pltpu.VMEM((ec, tt), MXU_DTYPE),
                        pltpu.VMEM((2, PEER_HEADS, n_i, tt), jnp.uint32),
                        pltpu.VMEM((2, PEER_HEADS, PEER_N_KEYS, tt), GATE_DTYPE)],
        compiler_params=_params(("parallel", "parallel", "arbitrary"), 58),
        name="peer_mix",
    )(hpT, e_down, e_upT, nb, p1, r2, p2, x2T, gfin)


def _rope_tables(s):
    rows = s // GRID_W
    row_pos = jnp.repeat(jnp.arange(rows, dtype=jnp.float32), GRID_W)
    col_pos = jnp.tile(jnp.arange(GRID_W, dtype=jnp.float32), rows)
    n_freq = HEAD_DIM // 4
    inv_freq = ROPE_THETA ** (-jnp.arange(n_freq, dtype=jnp.float32) / n_freq)
    ang_r = inv_freq[:, None] * row_pos[None, :]
    ang_c = inv_freq[:, None] * col_pos[None, :]
    cos_t = jnp.concatenate([jnp.cos(ang_r), jnp.cos(ang_r), jnp.cos(ang_c), jnp.cos(ang_c)], axis=0)
    sin_t = jnp.concatenate([-jnp.sin(ang_r), jnp.sin(ang_r), -jnp.sin(ang_c), jnp.sin(ang_c)], axis=0)
    return cos_t, sin_t


def _tiles(s):
    return dict(inproj=min(512, s), attn=256, merge=min(512, s), sel=256, mix=min(1024, s))


def _layer(x, kmem, vmemT, w):
    s = x.shape[1]
    t = _tiles(s)
    cos_t, sin_t = _rope_tables(s)
    qT, k, vT, gattnT, sgpT = _inproj(x, w["gmix"], w["w_inT"], w["gq"], w["gk"], cos_t, sin_t, w["gsg"],
                                      w["wsT"], w["bs"], w["wsgoT"], t["inproj"])
    attnT = _attention(qT, k, vT, t["attn"])
    x2T, hpT = _merge(x, attnT, gattnT, sgpT, w["waoT"], w["woutT"], w["gxa"], w["wqxT"], kmem, vmemT,
                      w["woxT"], w["gffn"], t["merge"])
    nb, p1, r2, p2 = _peer_sel(hpT, w["wpqT"], w["k1"], w["k2"], t["sel"])
    return _peer_mix(hpT, w["e_down"], w["e_upT"], nb, p1, r2, p2, x2T, w["gfin"], t["mix"])


def kernel(x_prompt, x_sample, mem_prompt, mem_sample, norm_mix_g, w_in, q_norm_g, k_norm_g, sg_norm_g, sg_w,
           sg_b, w_attn_o, w_sg_o, w_out, norm_xa_g, norm_mem_g, wq_xa, wkv_xa, wo_xa, norm_ffn_g, w_peer_q,
           peer_k1, peer_k2, expert_down, expert_up, final_norm_g):
    assert w_in.shape[0] == 1, "single-layer trunk"
    w = _prep_weights(norm_mix_g[0], w_in[0], q_norm_g[0], k_norm_g[0], sg_norm_g[0], sg_w[0], sg_b[0],
                      w_attn_o[0], w_sg_o[0], w_out[0], norm_xa_g[0], norm_mem_g[0], wq_xa[0], wkv_xa[0],
                      wo_xa[0], norm_ffn_g[0], w_peer_q[0], peer_k1[0], peer_k2[0], expert_down[0],
                      expert_up[0], final_norm_g)
    outs = []
    for x, mem in ((x_prompt, mem_prompt), (x_sample, mem_sample)):
        kmem, vmemT = _kv_mem(mem, w["gmem"], w["wk"], w["wvT"])
        outs.append(_layer(x, kmem, vmemT, w))
    return tuple(outs)


def _prep_weights(norm_mix_g, w_in, q_norm_g, k_norm_g, sg_norm_g, sg_w, sg_b, w_attn_o, w_sg_o, w_out,
                  norm_xa_g, norm_mem_g, wq_xa, wkv_xa, wo_xa, norm_ffn_g, w_peer_q, peer_k1, peer_k2,
                  expert_down, expert_up, final_norm_g):
    cast_t = lambda a: a.T.astype(MXU_DTYPE)
    col = lambda g: g.reshape(-1, 1).astype(jnp.float32)
    n_e, d = expert_up.shape
    e_upT = jnp.swapaxes(expert_up.reshape(n_e // PEER_CHUNK, PEER_CHUNK, d), 1, 2).astype(MXU_DTYPE)
    return dict(
        gmix=norm_mix_g.reshape(1, -1), w_inT=cast_t(w_in), gq=col(q_norm_g), gk=col(k_norm_g),
        gsg=col(sg_norm_g), wsT=jnp.swapaxes(sg_w, 1, 2).astype(MXU_DTYPE),
        bs=sg_b.reshape(SG_GROUPS, 1, SG_CHUNK), wsgoT=cast_t(w_sg_o),
        waoT=cast_t(w_attn_o), woutT=cast_t(w_out), gxa=col(norm_xa_g), wqxT=cast_t(wq_xa),
        woxT=cast_t(wo_xa), gffn=col(norm_ffn_g), wpqT=cast_t(w_peer_q),
        k1=peer_k1.astype(MXU_DTYPE), k2=peer_k2.astype(MXU_DTYPE),
        e_down=expert_down.astype(MXU_DTYPE), e_upT=e_upT, gfin=col(final_norm_g),
        gmem=norm_mem_g.reshape(1, -1), wk=wkv_xa[:, :D_MODEL].astype(MXU_DTYPE), wvT=cast_t(wkv_xa[:, D_MODEL:]),
    )
```

```python
import functools
import math

import jax
import jax.numpy as jnp
from jax import lax
from jax.experimental import pallas as pl
from jax.experimental.pallas import tpu as pltpu

D_MODEL = 1024
GRID_W = 64
N_HEADS = 16
N_KV_HEADS = 4
HEAD_DIM = 64
KV_GROUP = N_HEADS // N_KV_HEADS
ATTN_Q_WIDTH = N_HEADS * HEAD_DIM
ATTN_KV_WIDTH = N_KV_HEADS * HEAD_DIM
ROPE_THETA = 10000.0
SG_WIDTH = 1024
SG_GROUPS = 8
SG_GROUP_DIM = SG_WIDTH // SG_GROUPS
SG_CHUNK = 128
XA_HEADS = 4
XA_HEAD_DIM = D_MODEL // XA_HEADS
PEER_HEADS = 8
PEER_N_KEYS = 128
PEER_D_KEY = 256
PEER_HALF = PEER_D_KEY // 2
PEER_TOPK = 16
PEER_CHUNK = 1024
EPS = 1e-6

_Q0 = 0
_K0 = _Q0 + ATTN_Q_WIDTH
_V0 = _K0 + ATTN_KV_WIDTH
_Z0 = _V0 + ATTN_KV_WIDTH
_G0 = _Z0 + 2 * SG_WIDTH
_IN_WIDTH = _G0 + 2 * D_MODEL

LANES = 128
SUBLANES = 8
V7X_VMEM_BYTES = 64 * 1024 * 1024

MXU_DTYPE = jnp.bfloat16
GATE_DTYPE = jnp.bfloat16
GATE_ROWS = 2 * SUBLANES
NEG_INF = float("-inf")


def _dot(a, b):
    return jnp.dot(a, b, preferred_element_type=jnp.float32)


def _dot_nt(a, b):
    return lax.dot_general(a, b, (((1,), (1,)), ((), ())), preferred_element_type=jnp.float32)


def _gelu(x):
    c = math.sqrt(2.0 / math.pi)
    return 0.5 * x * (1.0 + jnp.tanh(c * (x + 0.044715 * (x * x * x))))


def _sigmoid(x):
    return 1.0 / (1.0 + jnp.exp(-x))


def _rmsnorm_rows(x, g_row):
    ms = jnp.mean(x * x, axis=-1, keepdims=True)
    return x * lax.rsqrt(ms + EPS) * g_row


def _rmsnorm_cols(xT, g_col):
    ms = jnp.mean(xT * xT, axis=0, keepdims=True)
    return xT * lax.rsqrt(ms + EPS) * g_col


def _params(semantics, vmem_mb, flags=None):
    return pltpu.CompilerParams(dimension_semantics=semantics, vmem_limit_bytes=vmem_mb * 1024 * 1024,
                                flags=flags)


def _const_spec(shape):
    nd = len(shape)
    return pl.BlockSpec(shape, lambda *_: (0,) * nd)


def _kv_mem_kernel(mem_ref, g_ref, wk_ref, wvT_ref, k_ref, vT_ref):
    mn = _rmsnorm_rows(mem_ref[0], g_ref[...]).astype(MXU_DTYPE)
    k_ref[0] = _dot(mn, wk_ref[...]).astype(k_ref.dtype)
    vT_ref[0] = _dot_nt(wvT_ref[...], mn).astype(vT_ref.dtype)


def _kv_mem(mem, g_row, wk, wvT):
    nb, m, d = mem.shape
    return pl.pallas_call(
        _kv_mem_kernel,
        out_shape=(jax.ShapeDtypeStruct((nb, m, d), MXU_DTYPE), jax.ShapeDtypeStruct((nb, d, m), MXU_DTYPE)),
        grid=(nb,),
        in_specs=[pl.BlockSpec((1, m, d), lambda b: (b, 0, 0)), _const_spec((1, d)),
                  _const_spec((d, d)), _const_spec((d, d))],
        out_specs=(pl.BlockSpec((1, m, d), lambda b: (b, 0, 0)), pl.BlockSpec((1, d, m), lambda b: (b, 0, 0))),
        compiler_params=_params(("parallel",), 32),
        name="kv_mem",
    )(mem, g_row, wk, wvT)


def _head_norm_rope(t, g_col, cos, sin):
    ms = jnp.mean(t * t, axis=0, keepdims=True)
    t = t * lax.rsqrt(ms + EPS) * g_col
    q4 = HEAD_DIM // 4
    sw = jnp.concatenate([t[q4:2 * q4], t[0:q4], t[3 * q4:], t[2 * q4:3 * q4]], axis=0)
    return t * cos + sw * sin


def _inproj_kernel(x_ref, gmix_ref, w_ref, gq_ref, gk_ref, cos_ref, sin_ref, gsg_ref, wsT_ref, bs_ref, wsgo_ref,
                   qT_ref, k_ref, vT_ref, gattn_ref, sgp_ref):
    tm = x_ref.shape[1]
    h = _rmsnorm_rows(x_ref[0], gmix_ref[...]).astype(MXU_DTYPE)

    def proj_t(lo, hi):
        return _dot_nt(w_ref[lo:hi, :], h)

    cos = cos_ref[...]
    sin = sin_ref[...]

    q_t = proj_t(_Q0, _K0)
    scale = math.log2(math.e) / math.sqrt(HEAD_DIM)
    for hd in range(N_HEADS):
        r = _head_norm_rope(q_t[hd * HEAD_DIM:(hd + 1) * HEAD_DIM], gq_ref[...], cos, sin) * scale
        qT_ref[0, hd * HEAD_DIM:(hd + 1) * HEAD_DIM, :] = r.astype(qT_ref.dtype)

    k_t = proj_t(_K0, _V0)
    k_rot = jnp.concatenate(
        [_head_norm_rope(k_t[g * HEAD_DIM:(g + 1) * HEAD_DIM], gk_ref[...], cos, sin) for g in range(N_KV_HEADS)],
        axis=0)
    k_tok = k_rot.T
    for g in range(N_KV_HEADS):
        k_ref[0, g] = k_tok[:, g * HEAD_DIM:(g + 1) * HEAD_DIM].astype(k_ref.dtype)

    vT_ref[0] = proj_t(_V0, _Z0).astype(vT_ref.dtype)

    z_t = _gelu(proj_t(_Z0, _G0))
    u = z_t[:SG_WIDTH]
    vn = _rmsnorm_cols(z_t[SG_WIDTH:], gsg_ref[...]).astype(MXU_DTYPE)
    n_chunks = tm // SG_CHUNK
    sv_groups = []
    for g in range(SG_GROUPS):
        vg = vn[g * SG_GROUP_DIM:(g + 1) * SG_GROUP_DIM]
        lhs = jnp.concatenate([vg[:, c * SG_CHUNK:(c + 1) * SG_CHUNK] for c in range(n_chunks)], axis=0)
        r = _dot(lhs, wsT_ref[g]) + bs_ref[g]
        sv_groups.append(jnp.concatenate(
            [r[c * SG_GROUP_DIM:(c + 1) * SG_GROUP_DIM] for c in range(n_chunks)], axis=1))
    sg = (u * jnp.concatenate(sv_groups, axis=0)).astype(MXU_DTYPE)
    sg_branch = _dot(wsgo_ref[...], sg)

    gates = _sigmoid(proj_t(_G0, _IN_WIDTH))
    gattn_ref[0] = gates[:D_MODEL].astype(gattn_ref.dtype)
    sgp_ref[0] = (gates[D_MODEL:] * sg_branch).astype(sgp_ref.dtype)


def _inproj(x, gmix, w_inT, gq, gk, cos_t, sin_t, gsg, wsT, bs, wsgoT, tm):
    b, s, d = x.shape
    grid = (b, s // tm)
    tok = lambda rows: pl.BlockSpec((1, rows, tm), lambda i, j: (i, 0, j))
    return pl.pallas_call(
        _inproj_kernel,
        out_shape=(jax.ShapeDtypeStruct((b, ATTN_Q_WIDTH, s), MXU_DTYPE),
                   jax.ShapeDtypeStruct((b, N_KV_HEADS, s, HEAD_DIM), MXU_DTYPE),
                   jax.ShapeDtypeStruct((b, ATTN_KV_WIDTH, s), MXU_DTYPE),
                   jax.ShapeDtypeStruct((b, D_MODEL, s), MXU_DTYPE),
                   jax.ShapeDtypeStruct((b, D_MODEL, s), jnp.float32)),
        grid=grid,
        in_specs=[pl.BlockSpec((1, tm, d), lambda i, j: (i, j, 0)),
                  _const_spec((1, d)), _const_spec(w_inT.shape),
                  _const_spec((HEAD_DIM, 1)), _const_spec((HEAD_DIM, 1)),
                  pl.BlockSpec((HEAD_DIM, tm), lambda i, j: (0, j)),
                  pl.BlockSpec((HEAD_DIM, tm), lambda i, j: (0, j)),
                  _const_spec((SG_WIDTH, 1)), _const_spec(wsT.shape), _const_spec(bs.shape),
                  _const_spec(wsgoT.shape)],
        out_specs=(tok(ATTN_Q_WIDTH),
                   pl.BlockSpec((1, N_KV_HEADS, tm, HEAD_DIM), lambda i, j: (i, 0, j, 0)),
                   tok(ATTN_KV_WIDTH), tok(D_MODEL), tok(D_MODEL)),
        compiler_params=_params(("parallel", "parallel"), 56),
        name="inproj",
    )(x, gmix, w_inT, gq, gk, cos_t, sin_t, gsg, wsT, bs, wsgoT)


def _attn_kernel(qT_ref, k_ref, vT_ref, o_ref):
    tq = qT_ref.shape[2]
    k = k_ref[0, 0]
    vT = vT_ref[0]
    vT1 = jnp.concatenate([vT, jnp.ones((GATE_ROWS, vT.shape[1]), vT.dtype)], axis=0)
    for pair in range(KV_GROUP // 2):
        heads = (2 * pair, 2 * pair + 1)
        qT2 = jnp.concatenate([qT_ref[0, a * HEAD_DIM:(a + 1) * HEAD_DIM, :] for a in heads], axis=1)
        sT = _dot(k, qT2).astype(MXU_DTYPE)
        m = jnp.max(sT, axis=0, keepdims=True)
        p = jnp.exp2(sT - m)
        o = _dot(vT1, p)
        oT = o[:HEAD_DIM] / o[HEAD_DIM:HEAD_DIM + 1]
        for i, a in enumerate(heads):
            o_ref[0, a * HEAD_DIM:(a + 1) * HEAD_DIM, :] = oT[:, i * tq:(i + 1) * tq].astype(o_ref.dtype)


def _attention(qT, k, vT, tq):
    b, _, s = qT.shape
    rows = KV_GROUP * HEAD_DIM
    return pl.pallas_call(
        _attn_kernel,
        out_shape=jax.ShapeDtypeStruct((b, ATTN_Q_WIDTH, s), MXU_DTYPE),
        grid=(b, N_KV_HEADS, s // tq),
        in_specs=[pl.BlockSpec((1, rows, tq), lambda i, g, j: (i, g, j)),
                  pl.BlockSpec((1, 1, s, HEAD_DIM), lambda i, g, j: (i, g, 0, 0)),
                  pl.BlockSpec((1, HEAD_DIM, s), lambda i, g, j: (i, g, 0))],
        out_specs=pl.BlockSpec((1, rows, tq), lambda i, g, j: (i, g, j)),
        compiler_params=_params(("parallel", "parallel", "parallel"), 48),
        name="attn",
    )(qT, k, vT)


def _merge_kernel(x_ref, attn_ref, gattn_ref, sgp_ref, wao_ref, wout_ref, gxa_ref, wq_ref, km_ref, vmT_ref,
                  wo_ref, gffn_ref, x2_ref, hp_ref):
    xT = x_ref[0].T
    attn_branch = _dot(wao_ref[...], attn_ref[0])
    mix = gattn_ref[0].astype(jnp.float32) * attn_branch + sgp_ref[0]
    x1 = xT + _dot(wout_ref[...], mix.astype(MXU_DTYPE))

    hx = _rmsnorm_cols(x1, gxa_ref[...]).astype(MXU_DTYPE)
    qx = (_dot(wq_ref[...], hx) * (1.0 / math.sqrt(XA_HEAD_DIM))).astype(MXU_DTYPE)
    outs = []
    for hd in range(XA_HEADS):
        lo, hi = hd * XA_HEAD_DIM, (hd + 1) * XA_HEAD_DIM
        sT = _dot(km_ref[0, :, lo:hi], qx[lo:hi])
        m = jnp.max(sT, axis=0, keepdims=True)
        p = jnp.exp(sT - m)
        l = jnp.sum(p, axis=0, keepdims=True)
        outs.append((_dot(vmT_ref[0, lo:hi, :], p.astype(MXU_DTYPE)) / l).astype(MXU_DTYPE))
    x2 = x1 + _dot(wo_ref[...], jnp.concatenate(outs, axis=0))
    x2_ref[0] = x2
    hp_ref[0] = _rmsnorm_cols(x2, gffn_ref[...]).astype(hp_ref.dtype)


def _merge(x, attnT, gattnT, sgpT, waoT, woutT, gxa, wqT, kmem, vmemT, woT, gffn, tm):
    b, s, d = x.shape
    m = kmem.shape[1]
    tok = pl.BlockSpec((1, d, tm), lambda i, j: (i, 0, j))
    wspec = _const_spec((d, d))
    col = _const_spec((d, 1))
    return pl.pallas_call(
        _merge_kernel,
        out_shape=(jax.ShapeDtypeStruct((b, d, s), jnp.float32), jax.ShapeDtypeStruct((b, d, s), MXU_DTYPE)),
        grid=(b, s // tm),
        in_specs=[pl.BlockSpec((1, tm, d), lambda i, j: (i, j, 0)), tok, tok, tok, wspec, wspec, col, wspec,
                  pl.BlockSpec((1, m, d), lambda i, j: (i, 0, 0)), pl.BlockSpec((1, d, m), lambda i, j: (i, 0, 0)),
                  wspec, col],
        out_specs=(tok, tok),
        compiler_params=_params(("parallel", "parallel"), 48),
        name="merge",
    )(x, attnT, gattnT, sgpT, waoT, woutT, gxa, wqT, kmem, vmemT, woT, gffn)


def _top_values(s, top_ref):
    cur = s
    rank = jnp.full(s.shape, float(PEER_TOPK), jnp.float32)
    for r in range(PEER_TOPK):
        mx = jnp.max(cur, axis=0, keepdims=True)
        top_ref[r:r + 1, :] = mx
        hit = cur == mx
        rank = jnp.where(hit, float(r), rank)
        if r + 1 < PEER_TOPK:
            cur = jnp.where(hit, NEG_INF, cur)
    return rank


def _stair_candidates(v1, v2):
    half = PEER_TOPK // 2
    row8 = lax.broadcasted_iota(jnp.int32, (half, 1), 0)
    cands = [v1 + v2[0:1], v1[0:1] + v2[half:]]
    for bb in range(1, half):
        a_max = PEER_TOPK // (bb + 1) - 1
        c = v1[:half] + v2[bb:bb + 1]
        cands.append(jnp.where(row8 <= a_max, c, NEG_INF))
    return jnp.concatenate(cands, axis=0)


def _peer_sel_kernel(hp_ref, wq_ref, k1_ref, k2_ref, nb_ref, p1_ref, r2_ref, p2_ref, v1_s, v2_s):
    qp = _dot(wq_ref[...], hp_ref[0]).astype(MXU_DTYPE)
    for hd in range(PEER_HEADS):
        base = hd * PEER_D_KEY
        s1 = _dot(k1_ref[hd], qp[base:base + PEER_HALF])
        s2 = _dot(k2_ref[hd], qp[base + PEER_HALF:base + PEER_D_KEY])
        r1 = _top_values(s1, v1_s)
        r2 = _top_values(s2, v2_s)
        v1 = v1_s[...]
        v2 = v2_s[...]
        cand = _stair_candidates(v1, v2)
        cur = cand
        for r in range(PEER_TOPK):
            t = jnp.max(cur, axis=0, keepdims=True)
            if r + 1 < PEER_TOPK:
                cur = jnp.where(cur == t, NEG_INF, cur)
        m1 = v1[0:1]
        m2 = v2[0:1]
        z = jnp.sum(jnp.where(cand >= t, jnp.exp(cand - (m1 + m2)), 0.0), axis=0, keepdims=True)
        nb_top = jnp.zeros_like(v1)
        for b in range(PEER_TOPK):
            nb_top = nb_top + jnp.where(v1 + v2[b:b + 1] >= t, 1.0, 0.0)
        nb = jnp.zeros_like(s1)
        for a in range(PEER_TOPK):
            nb = jnp.where(r1 == float(a), nb_top[a:a + 1], nb)
        nb_ref[0, hd] = _pair_words(nb)
        p1_ref[0, hd] = _pair_words(jnp.exp(s1 - m1) / z)
        r2_ref[0, hd] = r2.astype(r2_ref.dtype)
        p2_ref[0, hd] = jnp.exp(s2 - m2).astype(p2_ref.dtype)


def _pair_words(x):
    u = lax.bitcast_convert_type(x.astype(GATE_DTYPE).astype(jnp.float32), jnp.uint32)
    return u | (u >> 16)


def _peer_sel(hpT, wpqT, k1, k2, tm):
    b, d, s = hpT.shape
    shape = (b, PEER_HEADS, PEER_N_KEYS, s)
    words = jax.ShapeDtypeStruct(shape, jnp.uint32)
    sel = jax.ShapeDtypeStruct(shape, GATE_DTYPE)
    sel_spec = pl.BlockSpec((1, PEER_HEADS, PEER_N_KEYS, tm), lambda i, j: (i, 0, 0, j))
    return pl.pallas_call(
        _peer_sel_kernel,
        out_shape=(words, words, sel, sel),
        grid=(b, s // tm),
        in_specs=[pl.BlockSpec((1, d, tm), lambda i, j: (i, 0, j)), _const_spec(wpqT.shape),
                  _const_spec(k1.shape), _const_spec(k2.shape)],
        out_specs=(sel_spec, sel_spec, sel_spec, sel_spec),
        scratch_shapes=[pltpu.VMEM((PEER_TOPK, tm), jnp.float32), pltpu.VMEM((PEER_TOPK, tm), jnp.float32)],
        compiler_params=_params(("parallel", "parallel"), 48),
        name="peer_sel",
    )(hpT, wpqT, k1, k2)


def _peer_mix_kernel(hp_ref, ed_ref, euT_ref, nb_ref, p1_ref, r2_ref, p2_ref, x2_ref, gfin_ref, y_ref,
                     acc_ref, act_ref, w_ref, row_s, key_s):
    c = pl.program_id(2)
    n_c = pl.num_programs(2)
    tt = hp_ref.shape[2]
    n_i = ed_ref.shape[0] // PEER_N_KEYS
    jb = GATE_ROWS
    n_k = PEER_N_KEYS // jb

    @pl.when(c == 0)
    def _():
        acc_ref[...] = jnp.zeros_like(acc_ref)
        key_s[0] = r2_ref[0]
        key_s[1] = p2_ref[0]

    row_s[0] = nb_ref[0]
    row_s[1] = p1_ref[0]

    act_ref[...] = _gelu(_dot(ed_ref[...], hp_ref[0])).astype(act_ref.dtype)

    def row_bcast(words):
        return pltpu.bitcast(jnp.broadcast_to(words, (SUBLANES, LANES)), GATE_DTYPE)

    zero = jnp.zeros((jb, LANES), GATE_DTYPE)

    for lb in range(tt // LANES):
        ls = pl.ds(lb * LANES, LANES)
        for il in range(n_i):
            accs = [None] * n_k
            for hd in range(PEER_HEADS):
                bn = row_bcast(row_s[0, hd, il:il + 1, ls])
                bp = row_bcast(row_s[1, hd, il:il + 1, ls])
                for k in range(n_k):
                    r2v = key_s[0, hd, k * jb:(k + 1) * jb, ls]
                    p2v = key_s[1, hd, k * jb:(k + 1) * jb, ls]
                    term = bp.astype(jnp.float32) * jnp.where(r2v < bn, p2v, zero).astype(jnp.float32)
                    accs[k] = term if accs[k] is None else accs[k] + term
            for k in range(n_k):
                rows = pl.ds(il * PEER_N_KEYS + k * jb, jb)
                w_ref[rows, ls] = (accs[k] * act_ref[rows, ls].astype(jnp.float32)).astype(w_ref.dtype)

    acc_ref[...] += _dot(euT_ref[0], w_ref[...])

    @pl.when(c == n_c - 1)
    def _():
        x3 = x2_ref[0] + acc_ref[...]
        y_ref[0] = _rmsnorm_cols(x3, gfin_ref[...]).T.astype(y_ref.dtype)


def _peer_mix(hpT, e_down, e_upT, nb, p1, r2, p2, x2T, gfin, tt):
    b, d, s = hpT.shape
    n_e = e_down.shape[0]
    ec = PEER_CHUNK
    n_i = ec // PEER_N_KEYS
    tok = lambda rows: pl.BlockSpec((1, rows, tt), lambda i, j, c: (i, 0, j))
    row_spec = pl.BlockSpec((1, PEER_HEADS, n_i, tt), lambda i, j, c: (i, 0, c, j))
    full_spec = pl.BlockSpec((1, PEER_HEADS, PEER_N_KEYS, tt), lambda i, j, c: (i, 0, 0, j))
    return pl.pallas_call(
        _peer_mix_kernel,
        out_shape=jax.ShapeDtypeStruct((b, s, d), jnp.float32),
        grid=(b, s // tt, n_e // ec),
        in_specs=[tok(d),
                  pl.BlockSpec((ec, d), lambda i, j, c: (c, 0)),
                  pl.BlockSpec((1, d, ec), lambda i, j, c: (c, 0, 0)),
                  row_spec, row_spec, full_spec, full_spec, tok(d), _const_spec((d, 1))],
        out_specs=pl.BlockSpec((1, tt, d), lambda i, j, c: (i, j, 0)),
        scratch_shapes=[pltpu.VMEM((d, tt), jnp.float32), pltpu.VMEM((ec, tt), GATE_DTYPE),
                        pltpu.VMEM((ec, tt), MXU_DTYPE),
                        pltpu.VMEM((2, PEER_HEADS, n_i, tt), jnp.uint32),
                        pltpu.VMEM((2, PEER_HEADS, PEER_N_KEYS, tt), GATE_DTYPE)],
        compiler_params=_params(("parallel", "parallel", "arbitrary"), 56),
        name="peer_mix",
    )(hpT, e_down, e_upT, nb, p1, r2, p2, x2T, gfin)


def _rope_tables(s):
    rows = s // GRID_W
    row_pos = jnp.repeat(jnp.arange(rows, dtype=jnp.float32), GRID_W)
    col_pos = jnp.tile(jnp.arange(GRID_W, dtype=jnp.float32), rows)
    n_freq = HEAD_DIM // 4
    inv_freq = ROPE_THETA ** (-jnp.arange(n_freq, dtype=jnp.float32) / n_freq)
    ang_r = inv_freq[:, None] * row_pos[None, :]
    ang_c = inv_freq[:, None] * col_pos[None, :]
    cos_t = jnp.concatenate([jnp.cos(ang_r), jnp.cos(ang_r), jnp.cos(ang_c), jnp.cos(ang_c)], axis=0)
    sin_t = jnp.concatenate([-jnp.sin(ang_r), jnp.sin(ang_r), -jnp.sin(ang_c), jnp.sin(ang_c)], axis=0)
    return cos_t, sin_t


def _tiles(s):
    return dict(inproj=min(512, s), attn=256, merge=min(512, s), sel=256, mix=min(512, s))


def _layer(x, kmem, vmemT, w):
    s = x.shape[1]
    t = _tiles(s)
    cos_t, sin_t = _rope_tables(s)
    qT, k, vT, gattnT, sgpT = _inproj(x, w["gmix"], w["w_inT"], w["gq"], w["gk"], cos_t, sin_t, w["gsg"],
                                      w["wsT"], w["bs"], w["wsgoT"], t["inproj"])
    attnT = _attention(qT, k, vT, t["attn"])
    x2T, hpT = _merge(x, attnT, gattnT, sgpT, w["waoT"], w["woutT"], w["gxa"], w["wqxT"], kmem, vmemT,
                      w["woxT"], w["gffn"], t["merge"])
    nb, p1, r2, p2 = _peer_sel(hpT, w["wpqT"], w["k1"], w["k2"], t["sel"])
    return _peer_mix(hpT, w["e_down"], w["e_upT"], nb, p1, r2, p2, x2T, w["gfin"], t["mix"])


def kernel(x_prompt, x_sample, mem_prompt, mem_sample, norm_mix_g, w_in, q_norm_g, k_norm_g, sg_norm_g, sg_w,
           sg_b, w_attn_o, w_sg_o, w_out, norm_xa_g, norm_mem_g, wq_xa, wkv_xa, wo_xa, norm_ffn_g, w_peer_q,
           peer_k1, peer_k2, expert_down, expert_up, final_norm_g):
    assert w_in.shape[0] == 1, "single-layer trunk"
    w = _prep_weights(norm_mix_g[0], w_in[0], q_norm_g[0], k_norm_g[0], sg_norm_g[0], sg_w[0], sg_b[0],
                      w_attn_o[0], w_sg_o[0], w_out[0], norm_xa_g[0], norm_mem_g[0], wq_xa[0], wkv_xa[0],
                      wo_xa[0], norm_ffn_g[0], w_peer_q[0], peer_k1[0], peer_k2[0], expert_down[0],
                      expert_up[0], final_norm_g)
    outs = []
    for x, mem in ((x_prompt, mem_prompt), (x_sample, mem_sample)):
        kmem, vmemT = _kv_mem(mem, w["gmem"], w["wk"], w["wvT"])
        outs.append(_layer(x, kmem, vmemT, w))
    return tuple(outs)


def _prep_weights(norm_mix_g, w_in, q_norm_g, k_norm_g, sg_norm_g, sg_w, sg_b, w_attn_o, w_sg_o, w_out,
                  norm_xa_g, norm_mem_g, wq_xa, wkv_xa, wo_xa, norm_ffn_g, w_peer_q, peer_k1, peer_k2,
                  expert_down, expert_up, final_norm_g):
    cast_t = lambda a: a.T.astype(MXU_DTYPE)
    col = lambda g: g.reshape(-1, 1).astype(jnp.float32)
    n_e, d = expert_up.shape
    e_upT = jnp.swapaxes(expert_up.reshape(n_e // PEER_CHUNK, PEER_CHUNK, d), 1, 2).astype(MXU_DTYPE)
    return dict(
        gmix=norm_mix_g.reshape(1, -1), w_inT=cast_t(w_in), gq=col(q_norm_g), gk=col(k_norm_g),
        gsg=col(sg_norm_g), wsT=jnp.swapaxes(sg_w, 1, 2).astype(MXU_DTYPE),
        bs=sg_b.reshape(SG_GROUPS, 1, SG_CHUNK), wsgoT=cast_t(w_sg_o),
        waoT=cast_t(w_attn_o), woutT=cast_t(w_out), gxa=col(norm_xa_g), wqxT=cast_t(wq_xa),
        woxT=cast_t(wo_xa), gffn=col(norm_ffn_g), wpqT=cast_t(w_peer_q),
        k1=peer_k1.astype(MXU_DTYPE), k2=peer_k2.astype(MXU_DTYPE),
        e_down=expert_down.astype(MXU_DTYPE), e_upT=e_upT, gfin=col(final_norm_g),
        gmem=norm_mem_g.reshape(1, -1), wk=wkv_xa[:, :D_MODEL].astype(MXU_DTYPE), wvT=cast_t(wkv_xa[:, D_MODEL:]),
    )
```

```python
import functools
import math

import jax
import jax.numpy as jnp
from jax import lax
from jax.experimental import pallas as pl
from jax.experimental.pallas import tpu as pltpu

D_MODEL = 1024
GRID_W = 64
N_HEADS = 16
N_KV_HEADS = 4
HEAD_DIM = 64
KV_GROUP = N_HEADS // N_KV_HEADS
ATTN_Q_WIDTH = N_HEADS * HEAD_DIM
ATTN_KV_WIDTH = N_KV_HEADS * HEAD_DIM
ROPE_THETA = 10000.0
SG_WIDTH = 1024
SG_GROUPS = 8
SG_GROUP_DIM = SG_WIDTH // SG_GROUPS
SG_CHUNK = 128
XA_HEADS = 4
XA_HEAD_DIM = D_MODEL // XA_HEADS
PEER_HEADS = 8
PEER_N_KEYS = 128
PEER_D_KEY = 256
PEER_HALF = PEER_D_KEY // 2
PEER_TOPK = 16
PEER_CHUNK = 1024
EPS = 1e-6

_Q0 = 0
_K0 = _Q0 + ATTN_Q_WIDTH
_V0 = _K0 + ATTN_KV_WIDTH
_Z0 = _V0 + ATTN_KV_WIDTH
_G0 = _Z0 + 2 * SG_WIDTH
_IN_WIDTH = _G0 + 2 * D_MODEL

LANES = 128
SUBLANES = 8
V7X_VMEM_BYTES = 64 * 1024 * 1024

MXU_DTYPE = jnp.bfloat16
GATE_DTYPE = jnp.bfloat16
GATE_ROWS = 2 * SUBLANES
NEG_INF = float("-inf")


def _dot(a, b):
    return jnp.dot(a, b, preferred_element_type=jnp.float32)


def _dot_nt(a, b):
    return lax.dot_general(a, b, (((1,), (1,)), ((), ())), preferred_element_type=jnp.float32)


def _gelu(x):
    c = math.sqrt(2.0 / math.pi)
    return 0.5 * x * (1.0 + jnp.tanh(c * (x + 0.044715 * (x * x * x))))


def _sigmoid(x):
    return 1.0 / (1.0 + jnp.exp(-x))


def _rmsnorm_rows(x, g_row):
    ms = jnp.mean(x * x, axis=-1, keepdims=True)
    return x * lax.rsqrt(ms + EPS) * g_row


def _rmsnorm_cols(xT, g_col):
    ms = jnp.mean(xT * xT, axis=0, keepdims=True)
    return xT * lax.rsqrt(ms + EPS) * g_col


def _params(semantics, vmem_mb, flags=None):
    return pltpu.CompilerParams(dimension_semantics=semantics, vmem_limit_bytes=vmem_mb * 1024 * 1024,
                                flags=flags)


def _const_spec(shape):
    nd = len(shape)
    return pl.BlockSpec(shape, lambda *_: (0,) * nd)


def _kv_mem_kernel(mem_ref, g_ref, wk_ref, wvT_ref, k_ref, vT_ref):
    mn = _rmsnorm_rows(mem_ref[0], g_ref[...]).astype(MXU_DTYPE)
    k_ref[0] = _dot(mn, wk_ref[...]).astype(k_ref.dtype)
    vT_ref[0] = _dot_nt(wvT_ref[...], mn).astype(vT_ref.dtype)


def _kv_mem(mem, g_row, wk, wvT):
    nb, m, d = mem.shape
    return pl.pallas_call(
        _kv_mem_kernel,
        out_shape=(jax.ShapeDtypeStruct((nb, m, d), MXU_DTYPE), jax.ShapeDtypeStruct((nb, d, m), MXU_DTYPE)),
        grid=(nb,),
        in_specs=[pl.BlockSpec((1, m, d), lambda b: (b, 0, 0)), _const_spec((1, d)),
                  _const_spec((d, d)), _const_spec((d, d))],
        out_specs=(pl.BlockSpec((1, m, d), lambda b: (b, 0, 0)), pl.BlockSpec((1, d, m), lambda b: (b, 0, 0))),
        compiler_params=_params(("parallel",), 32),
        name="kv_mem",
    )(mem, g_row, wk, wvT)


def _head_norm_rope(t, g_col, cos, sin):
    ms = jnp.mean(t * t, axis=0, keepdims=True)
    t = t * lax.rsqrt(ms + EPS) * g_col
    q4 = HEAD_DIM // 4
    sw = jnp.concatenate([t[q4:2 * q4], t[0:q4], t[3 * q4:], t[2 * q4:3 * q4]], axis=0)
    return t * cos + sw * sin


def _inproj_kernel(x_ref, gmix_ref, w_ref, gq_ref, gk_ref, cos_ref, sin_ref, gsg_ref, wsT_ref, bs_ref, wsgo_ref,
                   qT_ref, k_ref, vT_ref, gattn_ref, sgp_ref):
    tm = x_ref.shape[1]
    h = _rmsnorm_rows(x_ref[0], gmix_ref[...]).astype(MXU_DTYPE)

    def proj_t(lo, hi):
        return _dot_nt(w_ref[lo:hi, :], h)

    cos = cos_ref[...]
    sin = sin_ref[...]

    q_t = proj_t(_Q0, _K0)
    scale = math.log2(math.e) / math.sqrt(HEAD_DIM)
    for hd in range(N_HEADS):
        r = _head_norm_rope(q_t[hd * HEAD_DIM:(hd + 1) * HEAD_DIM], gq_ref[...], cos, sin) * scale
        qT_ref[0, hd * HEAD_DIM:(hd + 1) * HEAD_DIM, :] = r.astype(qT_ref.dtype)

    k_t = proj_t(_K0, _V0)
    k_rot = jnp.concatenate(
        [_head_norm_rope(k_t[g * HEAD_DIM:(g + 1) * HEAD_DIM], gk_ref[...], cos, sin) for g in range(N_KV_HEADS)],
        axis=0)
    k_tok = k_rot.T
    for g in range(N_KV_HEADS):
        k_ref[0, g] = k_tok[:, g * HEAD_DIM:(g + 1) * HEAD_DIM].astype(k_ref.dtype)

    vT_ref[0] = proj_t(_V0, _Z0).astype(vT_ref.dtype)

    z_t = _gelu(proj_t(_Z0, _G0))
    u = z_t[:SG_WIDTH]
    vn = _rmsnorm_cols(z_t[SG_WIDTH:], gsg_ref[...]).astype(MXU_DTYPE)
    n_chunks = tm // SG_CHUNK
    sv_groups = []
    for g in range(SG_GROUPS):
        vg = vn[g * SG_GROUP_DIM:(g + 1) * SG_GROUP_DIM]
        lhs = jnp.concatenate([vg[:, c * SG_CHUNK:(c + 1) * SG_CHUNK] for c in range(n_chunks)], axis=0)
        r = _dot(lhs, wsT_ref[g]) + bs_ref[g]
        sv_groups.append(jnp.concatenate(
            [r[c * SG_GROUP_DIM:(c + 1) * SG_GROUP_DIM] for c in range(n_chunks)], axis=1))
    sg = (u * jnp.concatenate(sv_groups, axis=0)).astype(MXU_DTYPE)
    sg_branch = _dot(wsgo_ref[...], sg)

    gates = _sigmoid(proj_t(_G0, _IN_WIDTH))
    gattn_ref[0] = gates[:D_MODEL].astype(gattn_ref.dtype)
    sgp_ref[0] = (gates[D_MODEL:] * sg_branch).astype(sgp_ref.dtype)


def _inproj(x, gmix, w_inT, gq, gk, cos_t, sin_t, gsg, wsT, bs, wsgoT, tm):
    b, s, d = x.shape
    grid = (b, s // tm)
    tok = lambda rows: pl.BlockSpec((1, rows, tm), lambda i, j: (i, 0, j))
    return pl.pallas_call(
        _inproj_kernel,
        out_shape=(jax.ShapeDtypeStruct((b, ATTN_Q_WIDTH, s), MXU_DTYPE),
                   jax.ShapeDtypeStruct((b, N_KV_HEADS, s, HEAD_DIM), MXU_DTYPE),
                   jax.ShapeDtypeStruct((b, ATTN_KV_WIDTH, s), MXU_DTYPE),
                   jax.ShapeDtypeStruct((b, D_MODEL, s), MXU_DTYPE),
                   jax.ShapeDtypeStruct((b, D_MODEL, s), jnp.float32)),
        grid=grid,
        in_specs=[pl.BlockSpec((1, tm, d), lambda i, j: (i, j, 0)),
                  _const_spec((1, d)), _const_spec(w_inT.shape),
                  _const_spec((HEAD_DIM, 1)), _const_spec((HEAD_DIM, 1)),
                  pl.BlockSpec((HEAD_DIM, tm), lambda i, j: (0, j)),
                  pl.BlockSpec((HEAD_DIM, tm), lambda i, j: (0, j)),
                  _const_spec((SG_WIDTH, 1)), _const_spec(wsT.shape), _const_spec(bs.shape),
                  _const_spec(wsgoT.shape)],
        out_specs=(tok(ATTN_Q_WIDTH),
                   pl.BlockSpec((1, N_KV_HEADS, tm, HEAD_DIM), lambda i, j: (i, 0, j, 0)),
                   tok(ATTN_KV_WIDTH), tok(D_MODEL), tok(D_MODEL)),
        compiler_params=_params(("parallel", "parallel"), 56),
        name="inproj",
    )(x, gmix, w_inT, gq, gk, cos_t, sin_t, gsg, wsT, bs, wsgoT)


def _attn_kernel(qT_ref, k_ref, vT_ref, o_ref):
    tq = qT_ref.shape[2]
    k = k_ref[0, 0]
    vT = vT_ref[0]
    vT1 = jnp.concatenate([vT, jnp.ones((GATE_ROWS, vT.shape[1]), vT.dtype)], axis=0)
    for pair in range(KV_GROUP // 2):
        heads = (2 * pair, 2 * pair + 1)
        qT2 = jnp.concatenate([qT_ref[0, a * HEAD_DIM:(a + 1) * HEAD_DIM, :] for a in heads], axis=1)
        sT = _dot(k, qT2).astype(MXU_DTYPE)
        m = jnp.max(sT, axis=0, keepdims=True)
        p = jnp.exp2(sT - m)
        o = _dot(vT1, p)
        oT = o[:HEAD_DIM] / o[HEAD_DIM:HEAD_DIM + 1]
        for i, a in enumerate(heads):
            o_ref[0, a * HEAD_DIM:(a + 1) * HEAD_DIM, :] = oT[:, i * tq:(i + 1) * tq].astype(o_ref.dtype)


def _attention(qT, k, vT, tq):
    b, _, s = qT.shape
    rows = KV_GROUP * HEAD_DIM
    return pl.pallas_call(
        _attn_kernel,
        out_shape=jax.ShapeDtypeStruct((b, ATTN_Q_WIDTH, s), MXU_DTYPE),
        grid=(b, N_KV_HEADS, s // tq),
        in_specs=[pl.BlockSpec((1, rows, tq), lambda i, g, j: (i, g, j)),
                  pl.BlockSpec((1, 1, s, HEAD_DIM), lambda i, g, j: (i, g, 0, 0)),
                  pl.BlockSpec((1, HEAD_DIM, s), lambda i, g, j: (i, g, 0))],
        out_specs=pl.BlockSpec((1, rows, tq), lambda i, g, j: (i, g, j)),
        compiler_params=_params(("parallel", "parallel", "parallel"), 48),
        name="attn",
    )(qT, k, vT)


def _merge_kernel(x_ref, attn_ref, gattn_ref, sgp_ref, wao_ref, wout_ref, gxa_ref, wq_ref, km_ref, vmT_ref,
                  wo_ref, gffn_ref, x2_ref, hp_ref):
    xT = x_ref[0].T
    attn_branch = _dot(wao_ref[...], attn_ref[0])
    mix = gattn_ref[0].astype(jnp.float32) * attn_branch + sgp_ref[0]
    x1 = xT + _dot(wout_ref[...], mix.astype(MXU_DTYPE))

    hx = _rmsnorm_cols(x1, gxa_ref[...]).astype(MXU_DTYPE)
    qx = (_dot(wq_ref[...], hx) * (1.0 / math.sqrt(XA_HEAD_DIM))).astype(MXU_DTYPE)
    outs = []
    for hd in range(XA_HEADS):
        lo, hi = hd * XA_HEAD_DIM, (hd + 1) * XA_HEAD_DIM
        sT = _dot(km_ref[0, :, lo:hi], qx[lo:hi])
        m = jnp.max(sT, axis=0, keepdims=True)
        p = jnp.exp(sT - m)
        l = jnp.sum(p, axis=0, keepdims=True)
        outs.append((_dot(vmT_ref[0, lo:hi, :], p.astype(MXU_DTYPE)) / l).astype(MXU_DTYPE))
    x2 = x1 + _dot(wo_ref[...], jnp.concatenate(outs, axis=0))
    x2_ref[0] = x2
    hp_ref[0] = _rmsnorm_cols(x2, gffn_ref[...]).astype(hp_ref.dtype)


def _merge(x, attnT, gattnT, sgpT, waoT, woutT, gxa, wqT, kmem, vmemT, woT, gffn, tm):
    b, s, d = x.shape
    m = kmem.shape[1]
    tok = pl.BlockSpec((1, d, tm), lambda i, j: (i, 0, j))
    wspec = _const_spec((d, d))
    col = _const_spec((d, 1))
    return pl.pallas_call(
        _merge_kernel,
        out_shape=(jax.ShapeDtypeStruct((b, d, s), jnp.float32), jax.ShapeDtypeStruct((b, d, s), MXU_DTYPE)),
        grid=(b, s // tm),
        in_specs=[pl.BlockSpec((1, tm, d), lambda i, j: (i, j, 0)), tok, tok, tok, wspec, wspec, col, wspec,
                  pl.BlockSpec((1, m, d), lambda i, j: (i, 0, 0)), pl.BlockSpec((1, d, m), lambda i, j: (i, 0, 0)),
                  wspec, col],
        out_specs=(tok, tok),
        compiler_params=_params(("parallel", "parallel"), 48),
        name="merge",
    )(x, attnT, gattnT, sgpT, waoT, woutT, gxa, wqT, kmem, vmemT, woT, gffn)


def _sort16_network():
    def merge(lo, hi, r):
        step = 2 * r
        if step < hi - lo:
            yield from merge(lo, hi, step)
            yield from merge(lo + r, hi, step)
            yield from ((i, i + r) for i in range(lo + r, hi - r, step))
        else:
            yield (lo, lo + r)

    def sort(lo, hi):
        if hi > lo:
            mid = lo + (hi - lo) // 2
            yield from sort(lo, mid)
            yield from sort(mid + 1, hi)
            yield from merge(lo, hi, 1)

    return tuple(sort(0, PEER_TOPK - 1))


def _merge16_network():
    out, k = [], PEER_TOPK // 2
    while k >= 1:
        out += [(i, i + k) for i in range(PEER_TOPK) if i % (2 * k) < k]
        k //= 2
    return tuple(out)


_SORT16 = _sort16_network()
_MERGE16 = _merge16_network()


def _exchange(x, net):
    for i, j in net:
        x[i], x[j] = jnp.maximum(x[i], x[j]), jnp.minimum(x[i], x[j])
    return x


def _top_values(s, top_ref):
    assert s.shape[0] == PEER_TOPK * SUBLANES
    x = _exchange([s[v * SUBLANES:(v + 1) * SUBLANES, :] for v in range(PEER_TOPK)], _SORT16)
    shift = SUBLANES // 2
    while shift >= 1:
        x = _exchange([jnp.maximum(x[v], pltpu.roll(x[PEER_TOPK - 1 - v], shift, 0)) for v in range(PEER_TOPK)],
                      _MERGE16)
        shift //= 2
    for r in range(PEER_TOPK):
        top_ref[r:r + 1, :] = x[r][0:1, :]


def _stair_candidates(v1, v2):
    half = PEER_TOPK // 2
    row8 = lax.broadcasted_iota(jnp.int32, (half, 1), 0)
    cands = [v1 + v2[0:1], v1[0:1] + v2[half:]]
    for bb in range(1, half):
        a_max = PEER_TOPK // (bb + 1) - 1
        c = v1[:half] + v2[bb:bb + 1]
        cands.append(jnp.where(row8 <= a_max, c, NEG_INF))
    return jnp.concatenate(cands, axis=0)


def _peer_sel_kernel(hp_ref, wq_ref, k1_ref, k2_ref, nb_ref, p1_ref, r2_ref, p2_ref, v1_s, v2_s, c_s):
    qp = _dot(wq_ref[...], hp_ref[0]).astype(MXU_DTYPE)
    for hd in range(PEER_HEADS):
        base = hd * PEER_D_KEY
        s1 = _dot(k1_ref[hd], qp[base:base + PEER_HALF])
        s2 = _dot(k2_ref[hd], qp[base + PEER_HALF:base + PEER_D_KEY])
        _top_values(s1, v1_s)
        _top_values(s2, v2_s)
        v1 = v1_s[...]
        v2 = v2_s[...]
        cand = _stair_candidates(v1, v2)
        pad = jnp.full((PEER_N_KEYS - cand.shape[0], cand.shape[1]), NEG_INF, jnp.float32)
        _top_values(jnp.concatenate([cand, pad], axis=0), c_s)
        t = c_s[PEER_TOPK - 1:PEER_TOPK, :]
        m1 = v1[0:1]
        m2 = v2[0:1]
        z = jnp.sum(jnp.where(cand >= t, jnp.exp(cand - (m1 + m2)), 0.0), axis=0, keepdims=True)
        nb_top = jnp.zeros_like(v1)
        for b in range(PEER_TOPK):
            nb_top = nb_top + jnp.where(v1 + v2[b:b + 1] >= t, 1.0, 0.0)
        nb = jnp.zeros_like(s1)
        r2 = jnp.full(s2.shape, float(PEER_TOPK), jnp.float32)
        for a in reversed(range(PEER_TOPK)):
            nb = jnp.where(s1 == v1[a:a + 1], nb_top[a:a + 1], nb)
            r2 = jnp.where(s2 == v2[a:a + 1], float(a), r2)
        nb_ref[0, hd] = _pair_words(nb)
        p1_ref[0, hd] = _pair_words(jnp.exp(s1 - m1) / z)
        r2_ref[0, hd] = r2.astype(r2_ref.dtype)
        p2_ref[0, hd] = jnp.exp(s2 - m2).astype(p2_ref.dtype)


def _pair_words(x):
    u = lax.bitcast_convert_type(x.astype(GATE_DTYPE).astype(jnp.float32), jnp.uint32)
    return u | (u >> 16)


def _peer_sel(hpT, wpqT, k1, k2, tm):
    b, d, s = hpT.shape
    shape = (b, PEER_HEADS, PEER_N_KEYS, s)
    words = jax.ShapeDtypeStruct(shape, jnp.uint32)
    sel = jax.ShapeDtypeStruct(shape, GATE_DTYPE)
    sel_spec = pl.BlockSpec((1, PEER_HEADS, PEER_N_KEYS, tm), lambda i, j: (i, 0, 0, j))
    return pl.pallas_call(
        _peer_sel_kernel,
        out_shape=(words, words, sel, sel),
        grid=(b, s // tm),
        in_specs=[pl.BlockSpec((1, d, tm), lambda i, j: (i, 0, j)), _const_spec(wpqT.shape),
                  _const_spec(k1.shape), _const_spec(k2.shape)],
        out_specs=(sel_spec, sel_spec, sel_spec, sel_spec),
        scratch_shapes=[pltpu.VMEM((PEER_TOPK, tm), jnp.float32) for _ in range(3)],
        compiler_params=_params(("parallel", "parallel"), 48),
        name="peer_sel",
    )(hpT, wpqT, k1, k2)


def _peer_mix_kernel(hp_ref, ed_ref, euT_ref, nb_ref, p1_ref, r2_ref, p2_ref, x2_ref, gfin_ref, y_ref,
                     acc_ref, act_ref, w_ref, row_s, key_s):
    c = pl.program_id(2)
    n_c = pl.num_programs(2)
    tt = hp_ref.shape[2]
    n_i = ed_ref.shape[0] // PEER_N_KEYS
    jb = GATE_ROWS
    n_k = PEER_N_KEYS // jb

    @pl.when(c == 0)
    def _():
        acc_ref[...] = jnp.zeros_like(acc_ref)
        key_s[0] = r2_ref[0]
        key_s[1] = p2_ref[0]

    row_s[0] = nb_ref[0]
    row_s[1] = p1_ref[0]

    act_ref[...] = _gelu(_dot(ed_ref[...], hp_ref[0])).astype(act_ref.dtype)

    def row_bcast(words):
        return pltpu.bitcast(jnp.broadcast_to(words, (SUBLANES, LANES)), GATE_DTYPE)

    zero = jnp.zeros((jb, LANES), GATE_DTYPE)

    for lb in range(tt // LANES):
        ls = pl.ds(lb * LANES, LANES)
        for il in range(n_i):
            accs = [None] * n_k
            for hd in range(PEER_HEADS):
                bn = row_bcast(row_s[0, hd, il:il + 1, ls])
                bp = row_bcast(row_s[1, hd, il:il + 1, ls])
                for k in range(n_k):
                    r2v = key_s[0, hd, k * jb:(k + 1) * jb, ls]
                    p2v = key_s[1, hd, k * jb:(k + 1) * jb, ls]
                    term = bp * jnp.where(r2v < bn, p2v, zero)
                    accs[k] = term if accs[k] is None else accs[k] + term
            for k in range(n_k):
                rows = pl.ds(il * PEER_N_KEYS + k * jb, jb)
                w_ref[rows, ls] = (accs[k] * act_ref[rows, ls]).astype(w_ref.dtype)

    acc_ref[...] += _dot(euT_ref[0], w_ref[...])

    @pl.when(c == n_c - 1)
    def _():
        x3 = x2_ref[0] + acc_ref[...]
        y_ref[0] = _rmsnorm_cols(x3, gfin_ref[...]).T.astype(y_ref.dtype)


def _peer_mix(hpT, e_down, e_upT, nb, p1, r2, p2, x2T, gfin, tt):
    b, d, s = hpT.shape
    n_e = e_down.shape[0]
    ec = PEER_CHUNK
    n_i = ec // PEER_N_KEYS
    tok = lambda rows: pl.BlockSpec((1, rows, tt), lambda i, j, c: (i, 0, j))
    row_spec = pl.BlockSpec((1, PEER_HEADS, n_i, tt), lambda i, j, c: (i, 0, c, j))
    full_spec = pl.BlockSpec((1, PEER_HEADS, PEER_N_KEYS, tt), lambda i, j, c: (i, 0, 0, j))
    return pl.pallas_call(
        _peer_mix_kernel,
        out_shape=jax.ShapeDtypeStruct((b, s, d), jnp.float32),
        grid=(b, s // tt, n_e // ec),
        in_specs=[tok(d),
                  pl.BlockSpec((ec, d), lambda i, j, c: (c, 0)),
                  pl.BlockSpec((1, d, ec), lambda i, j, c: (c, 0, 0)),
                  row_spec, row_spec, full_spec, full_spec, tok(d), _const_spec((d, 1))],
        out_specs=pl.BlockSpec((1, tt, d), lambda i, j, c: (i, j, 0)),
        scratch_shapes=[pltpu.VMEM((d, tt), jnp.float32), pltpu.VMEM((ec, tt), GATE_DTYPE),
                        pltpu.VMEM((ec, tt), MXU_DTYPE),
                        pltpu.VMEM((2, PEER_HEADS, n_i, tt), jnp.uint32),
                        pltpu.VMEM((2, PEER_HEADS, PEER_N_KEYS, tt), GATE_DTYPE)],
        compiler_params=_params(("parallel", "parallel", "arbitrary"), 56),
        name="peer_mix",
    )(hpT, e_down, e_upT, nb, p1, r2, p2, x2T, gfin)


def _rope_tables(s):
    rows = s // GRID_W
    row_pos = jnp.repeat(jnp.arange(rows, dtype=jnp.float32), GRID_W)
    col_pos = jnp.tile(jnp.arange(GRID_W, dtype=jnp.float32), rows)
    n_freq = HEAD_DIM // 4
    inv_freq = ROPE_THETA ** (-jnp.arange(n_freq, dtype=jnp.float32) / n_freq)
    ang_r = inv_freq[:, None] * row_pos[None, :]
    ang_c = inv_freq[:, None] * col_pos[None, :]
    cos_t = jnp.concatenate([jnp.cos(ang_r), jnp.cos(ang_r), jnp.cos(ang_c), jnp.cos(ang_c)], axis=0)
    sin_t = jnp.concatenate([-jnp.sin(ang_r), jnp.sin(ang_r), -jnp.sin(ang_c), jnp.sin(ang_c)], axis=0)
    return cos_t, sin_t


def _tiles(s):
    return dict(inproj=min(512, s), attn=256, merge=min(512, s), sel=256, mix=min(512, s))


def _layer(x, kmem, vmemT, w):
    s = x.shape[1]
    t = _tiles(s)
    cos_t, sin_t = _rope_tables(s)
    qT, k, vT, gattnT, sgpT = _inproj(x, w["gmix"], w["w_inT"], w["gq"], w["gk"], cos_t, sin_t, w["gsg"],
                                      w["wsT"], w["bs"], w["wsgoT"], t["inproj"])
    attnT = _attention(qT, k, vT, t["attn"])
    x2T, hpT = _merge(x, attnT, gattnT, sgpT, w["waoT"], w["woutT"], w["gxa"], w["wqxT"], kmem, vmemT,
                      w["woxT"], w["gffn"], t["merge"])
    nb, p1, r2, p2 = _peer_sel(hpT, w["wpqT"], w["k1"], w["k2"], t["sel"])
    return _peer_mix(hpT, w["e_down"], w["e_upT"], nb, p1, r2, p2, x2T, w["gfin"], t["mix"])


def kernel(x_prompt, x_sample, mem_prompt, mem_sample, norm_mix_g, w_in, q_norm_g, k_norm_g, sg_norm_g, sg_w,
           sg_b, w_attn_o, w_sg_o, w_out, norm_xa_g, norm_mem_g, wq_xa, wkv_xa, wo_xa, norm_ffn_g, w_peer_q,
           peer_k1, peer_k2, expert_down, expert_up, final_norm_g):
    assert w_in.shape[0] == 1, "single-layer trunk"
    w = _prep_weights(norm_mix_g[0], w_in[0], q_norm_g[0], k_norm_g[0], sg_norm_g[0], sg_w[0], sg_b[0],
                      w_attn_o[0], w_sg_o[0], w_out[0], norm_xa_g[0], norm_mem_g[0], wq_xa[0], wkv_xa[0],
                      wo_xa[0], norm_ffn_g[0], w_peer_q[0], peer_k1[0], peer_k2[0], expert_down[0],
                      expert_up[0], final_norm_g)
    outs = []
    for x, mem in ((x_prompt, mem_prompt), (x_sample, mem_sample)):
        kmem, vmemT = _kv_mem(mem, w["gmem"], w["wk"], w["wvT"])
        outs.append(_layer(x, kmem, vmemT, w))
    return tuple(outs)


def _prep_weights(norm_mix_g, w_in, q_norm_g, k_norm_g, sg_norm_g, sg_w, sg_b, w_attn_o, w_sg_o, w_out,
                  norm_xa_g, norm_mem_g, wq_xa, wkv_xa, wo_xa, norm_ffn_g, w_peer_q, peer_k1, peer_k2,
                  expert_down, expert_up, final_norm_g):
    cast_t = lambda a: a.T.astype(MXU_DTYPE)
    col = lambda g: g.reshape(-1, 1).astype(jnp.float32)
    n_e, d = expert_up.shape
    e_upT = jnp.swapaxes(expert_up.reshape(n_e // PEER_CHUNK, PEER_CHUNK, d), 1, 2).astype(MXU_DTYPE)
    return dict(
        gmix=norm_mix_g.reshape(1, -1), w_inT=cast_t(w_in), gq=col(q_norm_g), gk=col(k_norm_g),
        gsg=col(sg_norm_g), wsT=jnp.swapaxes(sg_w, 1, 2).astype(MXU_DTYPE),
        bs=sg_b.reshape(SG_GROUPS, 1, SG_CHUNK), wsgoT=cast_t(w_sg_o),
        waoT=cast_t(w_attn_o), woutT=cast_t(w_out), gxa=col(norm_xa_g), wqxT=cast_t(wq_xa),
        woxT=cast_t(wo_xa), gffn=col(norm_ffn_g), wpqT=cast_t(w_peer_q),
        k1=peer_k1.astype(MXU_DTYPE), k2=peer_k2.astype(MXU_DTYPE),
        e_down=expert_down.astype(MXU_DTYPE), e_upT=e_upT, gfin=col(final_norm_g),
        gmem=norm_mem_g.reshape(1, -1), wk=wkv_xa[:, :D_MODEL].astype(MXU_DTYPE), wvT=cast_t(wkv_xa[:, D_MODEL:]),
    )
```

```python
import math

import jax
import jax.numpy as jnp
from jax import lax
from jax.experimental import pallas as pl
from jax.experimental.pallas import tpu as pltpu

D_MODEL = 1024
GRID_W = 64
N_HEADS = 16
N_KV_HEADS = 4
HEAD_DIM = 64
KV_GROUP = N_HEADS // N_KV_HEADS
ATTN_Q_WIDTH = N_HEADS * HEAD_DIM
ATTN_KV_WIDTH = N_KV_HEADS * HEAD_DIM
ROPE_THETA = 10000.0
SG_WIDTH = 1024
SG_GROUPS = 8
SG_GROUP_DIM = SG_WIDTH // SG_GROUPS
SG_CHUNK = 128
XA_HEADS = 4
XA_HEAD_DIM = D_MODEL // XA_HEADS
PEER_HEADS = 8
PEER_N_KEYS = 128
PEER_D_KEY = 256
PEER_HALF = PEER_D_KEY // 2
PEER_TOPK = 16
PEER_CHUNK = 1024
EPS = 1e-6

_Q0 = 0
_K0 = _Q0 + ATTN_Q_WIDTH
_V0 = _K0 + ATTN_KV_WIDTH
_Z0 = _V0 + ATTN_KV_WIDTH
_G0 = _Z0 + 2 * SG_WIDTH
_IN_WIDTH = _G0 + 2 * D_MODEL

LANES = 128
SUBLANES = 8
V7X_VMEM_BYTES = 64 * 1024 * 1024
MIB = 1024 * 1024
ATTN_SCORE_BYTES = 8 * MIB
VMEM_LIMIT_MIB = dict(kv_mem=32, inproj=56, attn=48, merge=48, peer_sel=48, peer_mix=56)
assert max(VMEM_LIMIT_MIB.values()) * MIB < V7X_VMEM_BYTES

MXU_DTYPE = jnp.bfloat16
GATE_DTYPE = jnp.bfloat16
GATE_ROWS = 2 * SUBLANES
NEG_INF = float("-inf")


def _dot(a, b):
    return jnp.dot(a, b, preferred_element_type=jnp.float32)


def _dot_nt(a, b):
    return lax.dot_general(a, b, (((1,), (1,)), ((), ())), preferred_element_type=jnp.float32)


def _gelu(x):
    c = math.sqrt(2.0 / math.pi)
    return 0.5 * x * (1.0 + jnp.tanh(c * (x + 0.044715 * (x * x * x))))


def _sigmoid(x):
    return 1.0 / (1.0 + jnp.exp(-x))


def _rmsnorm_rows(x, g_row):
    ms = jnp.mean(x * x, axis=-1, keepdims=True)
    return x * lax.rsqrt(ms + EPS) * g_row


def _rmsnorm_cols(xT, g_col):
    ms = jnp.mean(xT * xT, axis=0, keepdims=True)
    return xT * lax.rsqrt(ms + EPS) * g_col


def _params(semantics, stage):
    return pltpu.CompilerParams(dimension_semantics=semantics, vmem_limit_bytes=VMEM_LIMIT_MIB[stage] * MIB)


def _const_spec(shape):
    nd = len(shape)
    return pl.BlockSpec(shape, lambda *_: (0,) * nd)


def _kv_mem_kernel(mem_ref, g_ref, wk_ref, wvT_ref, k_ref, vT_ref):
    mn = _rmsnorm_rows(mem_ref[0], g_ref[...]).astype(MXU_DTYPE)
    k_ref[0] = _dot(mn, wk_ref[...]).astype(k_ref.dtype)
    vT_ref[0] = _dot_nt(wvT_ref[...], mn).astype(vT_ref.dtype)


def _kv_mem(mem, g_row, wk, wvT):
    nb, m, d = mem.shape
    return pl.pallas_call(
        _kv_mem_kernel,
        out_shape=(jax.ShapeDtypeStruct((nb, m, d), MXU_DTYPE), jax.ShapeDtypeStruct((nb, d, m), MXU_DTYPE)),
        grid=(nb,),
        in_specs=[pl.BlockSpec((1, m, d), lambda b: (b, 0, 0)), _const_spec((1, d)),
                  _const_spec((d, d)), _const_spec((d, d))],
        out_specs=(pl.BlockSpec((1, m, d), lambda b: (b, 0, 0)), pl.BlockSpec((1, d, m), lambda b: (b, 0, 0))),
        compiler_params=_params(("parallel",), "kv_mem"),
        name="kv_mem",
    )(mem, g_row, wk, wvT)


def _head_norm_rope(t, g_col, cos, sin):
    ms = jnp.mean(t * t, axis=0, keepdims=True)
    t = t * lax.rsqrt(ms + EPS) * g_col
    q4 = HEAD_DIM // 4
    sw = jnp.concatenate([t[q4:2 * q4], t[0:q4], t[3 * q4:], t[2 * q4:3 * q4]], axis=0)
    return t * cos + sw * sin


def _inproj_kernel(x_ref, gmix_ref, w_ref, gq_ref, gk_ref, cos_ref, sin_ref, gsg_ref, wsT_ref, bs_ref, wsgo_ref,
                   qT_ref, k_ref, vT_ref, gattn_ref, sgp_ref):
    tm = x_ref.shape[1]
    h = _rmsnorm_rows(x_ref[0], gmix_ref[...]).astype(MXU_DTYPE)

    def proj_t(lo, hi):
        return _dot_nt(w_ref[lo:hi, :], h)

    cos = cos_ref[...]
    sin = sin_ref[...]

    q_t = proj_t(_Q0, _K0)
    scale = math.log2(math.e) / math.sqrt(HEAD_DIM)
    for hd in range(N_HEADS):
        r = _head_norm_rope(q_t[hd * HEAD_DIM:(hd + 1) * HEAD_DIM], gq_ref[...], cos, sin) * scale
        qT_ref[0, hd * HEAD_DIM:(hd + 1) * HEAD_DIM, :] = r.astype(qT_ref.dtype)

    k_t = proj_t(_K0, _V0)
    k_rot = jnp.concatenate(
        [_head_norm_rope(k_t[g * HEAD_DIM:(g + 1) * HEAD_DIM], gk_ref[...], cos, sin) for g in range(N_KV_HEADS)],
        axis=0)
    k_tok = k_rot.T
    for g in range(N_KV_HEADS):
        k_ref[0, g] = k_tok[:, g * HEAD_DIM:(g + 1) * HEAD_DIM].astype(k_ref.dtype)

    vT_ref[0] = proj_t(_V0, _Z0).astype(vT_ref.dtype)

    z_t = _gelu(proj_t(_Z0, _G0))
    u = z_t[:SG_WIDTH]
    vn = _rmsnorm_cols(z_t[SG_WIDTH:], gsg_ref[...]).astype(MXU_DTYPE)
    n_chunks = tm // SG_CHUNK
    sv_groups = []
    for g in range(SG_GROUPS):
        vg = vn[g * SG_GROUP_DIM:(g + 1) * SG_GROUP_DIM]
        lhs = jnp.concatenate([vg[:, c * SG_CHUNK:(c + 1) * SG_CHUNK] for c in range(n_chunks)], axis=0)
        r = _dot(lhs, wsT_ref[g]) + bs_ref[g]
        sv_groups.append(jnp.concatenate(
            [r[c * SG_GROUP_DIM:(c + 1) * SG_GROUP_DIM] for c in range(n_chunks)], axis=1))
    sg = (u * jnp.concatenate(sv_groups, axis=0)).astype(MXU_DTYPE)
    sg_branch = _dot(wsgo_ref[...], sg)

    gates = _sigmoid(proj_t(_G0, _IN_WIDTH))
    gattn_ref[0] = gates[:D_MODEL].astype(gattn_ref.dtype)
    sgp_ref[0] = (gates[D_MODEL:] * sg_branch).astype(sgp_ref.dtype)


def _inproj(x, gmix, w_inT, gq, gk, cos_t, sin_t, gsg, wsT, bs, wsgoT, tm):
    b, s, d = x.shape
    grid = (b, s // tm)
    tok = lambda rows: pl.BlockSpec((1, rows, tm), lambda i, j: (i, 0, j))
    return pl.pallas_call(
        _inproj_kernel,
        out_shape=(jax.ShapeDtypeStruct((b, ATTN_Q_WIDTH, s), MXU_DTYPE),
                   jax.ShapeDtypeStruct((b, N_KV_HEADS, s, HEAD_DIM), MXU_DTYPE),
                   jax.ShapeDtypeStruct((b, ATTN_KV_WIDTH, s), MXU_DTYPE),
                   jax.ShapeDtypeStruct((b, D_MODEL, s), MXU_DTYPE),
                   jax.ShapeDtypeStruct((b, D_MODEL, s), jnp.float32)),
        grid=grid,
        in_specs=[pl.BlockSpec((1, tm, d), lambda i, j: (i, j, 0)),
                  _const_spec((1, d)), _const_spec(w_inT.shape),
                  _const_spec((HEAD_DIM, 1)), _const_spec((HEAD_DIM, 1)),
                  pl.BlockSpec((HEAD_DIM, tm), lambda i, j: (0, j)),
                  pl.BlockSpec((HEAD_DIM, tm), lambda i, j: (0, j)),
                  _const_spec((SG_WIDTH, 1)), _const_spec(wsT.shape), _const_spec(bs.shape),
                  _const_spec(wsgoT.shape)],
        out_specs=(tok(ATTN_Q_WIDTH),
                   pl.BlockSpec((1, N_KV_HEADS, tm, HEAD_DIM), lambda i, j: (i, 0, j, 0)),
                   tok(ATTN_KV_WIDTH), tok(D_MODEL), tok(D_MODEL)),
        compiler_params=_params(("parallel", "parallel"), "inproj"),
        name="inproj",
    )(x, gmix, w_inT, gq, gk, cos_t, sin_t, gsg, wsT, bs, wsgoT)


def _attn_kernel(qT_ref, k_ref, vT_ref, o_ref):
    tq = qT_ref.shape[2]
    k = k_ref[0, 0]
    vT = vT_ref[0]
    vT1 = jnp.concatenate([vT, jnp.ones((GATE_ROWS, vT.shape[1]), vT.dtype)], axis=0)
    for pair in range(KV_GROUP // 2):
        heads = (2 * pair, 2 * pair + 1)
        qT2 = jnp.concatenate([qT_ref[0, a * HEAD_DIM:(a + 1) * HEAD_DIM, :] for a in heads], axis=1)
        sT = _dot(k, qT2).astype(MXU_DTYPE)
        m = jnp.max(sT, axis=0, keepdims=True)
        p = jnp.exp2(sT - m)
        o = _dot(vT1, p)
        oT = o[:HEAD_DIM] / o[HEAD_DIM:HEAD_DIM + 1]
        for i, a in enumerate(heads):
            o_ref[0, a * HEAD_DIM:(a + 1) * HEAD_DIM, :] = oT[:, i * tq:(i + 1) * tq].astype(o_ref.dtype)


def _attention(qT, k, vT, tq):
    b, _, s = qT.shape
    rows = KV_GROUP * HEAD_DIM
    return pl.pallas_call(
        _attn_kernel,
        out_shape=jax.ShapeDtypeStruct((b, ATTN_Q_WIDTH, s), MXU_DTYPE),
        grid=(b, N_KV_HEADS, s // tq),
        in_specs=[pl.BlockSpec((1, rows, tq), lambda i, g, j: (i, g, j)),
                  pl.BlockSpec((1, 1, s, HEAD_DIM), lambda i, g, j: (i, g, 0, 0)),
                  pl.BlockSpec((1, HEAD_DIM, s), lambda i, g, j: (i, g, 0))],
        out_specs=pl.BlockSpec((1, rows, tq), lambda i, g, j: (i, g, j)),
        compiler_params=_params(("parallel", "parallel", "parallel"), "attn"),
        name="attn",
    )(qT, k, vT)


def _merge_kernel(x_ref, attn_ref, gattn_ref, sgp_ref, wao_ref, wout_ref, gxa_ref, wq_ref, km_ref, vmT_ref,
                  wo_ref, gffn_ref, x2_ref, hp_ref):
    xT = x_ref[0].T
    attn_branch = _dot(wao_ref[...], attn_ref[0])
    mix = gattn_ref[0].astype(jnp.float32) * attn_branch + sgp_ref[0]
    x1 = xT + _dot(wout_ref[...], mix.astype(MXU_DTYPE))

    hx = _rmsnorm_cols(x1, gxa_ref[...]).astype(MXU_DTYPE)
    qx = (_dot(wq_ref[...], hx) * (1.0 / math.sqrt(XA_HEAD_DIM))).astype(MXU_DTYPE)
    outs = []
    for hd in range(XA_HEADS):
        lo, hi = hd * XA_HEAD_DIM, (hd + 1) * XA_HEAD_DIM
        sT = _dot(km_ref[0, :, lo:hi], qx[lo:hi])
        m = jnp.max(sT, axis=0, keepdims=True)
        p = jnp.exp(sT - m)
        l = jnp.sum(p, axis=0, keepdims=True)
        outs.append((_dot(vmT_ref[0, lo:hi, :], p.astype(MXU_DTYPE)) / l).astype(MXU_DTYPE))
    x2 = x1 + _dot(wo_ref[...], jnp.concatenate(outs, axis=0))
    x2_ref[0] = x2
    hp_ref[0] = _rmsnorm_cols(x2, gffn_ref[...]).astype(hp_ref.dtype)


def _merge(x, attnT, gattnT, sgpT, waoT, woutT, gxa, wqT, kmem, vmemT, woT, gffn, tm):
    b, s, d = x.shape
    m = kmem.shape[1]
    tok = pl.BlockSpec((1, d, tm), lambda i, j: (i, 0, j))
    wspec = _const_spec((d, d))
    col = _const_spec((d, 1))
    return pl.pallas_call(
        _merge_kernel,
        out_shape=(jax.ShapeDtypeStruct((b, d, s), jnp.float32), jax.ShapeDtypeStruct((b, d, s), MXU_DTYPE)),
        grid=(b, s // tm),
        in_specs=[pl.BlockSpec((1, tm, d), lambda i, j: (i, j, 0)), tok, tok, tok, wspec, wspec, col, wspec,
                  pl.BlockSpec((1, m, d), lambda i, j: (i, 0, 0)), pl.BlockSpec((1, d, m), lambda i, j: (i, 0, 0)),
                  wspec, col],
        out_specs=(tok, tok),
        compiler_params=_params(("parallel", "parallel"), "merge"),
        name="merge",
    )(x, attnT, gattnT, sgpT, waoT, woutT, gxa, wqT, kmem, vmemT, woT, gffn)


def _sort16_network():
    def merge(lo, hi, r):
        step = 2 * r
        if step < hi - lo:
            yield from merge(lo, hi, step)
            yield from merge(lo + r, hi, step)
            yield from ((i, i + r) for i in range(lo + r, hi - r, step))
        else:
            yield (lo, lo + r)

    def sort(lo, hi):
        if hi > lo:
            mid = lo + (hi - lo) // 2
            yield from sort(lo, mid)
            yield from sort(mid + 1, hi)
            yield from merge(lo, hi, 1)

    return tuple(sort(0, PEER_TOPK - 1))


def _merge16_network():
    out, k = [], PEER_TOPK // 2
    while k >= 1:
        out += [(i, i + k) for i in range(PEER_TOPK) if i % (2 * k) < k]
        k //= 2
    return tuple(out)


_SORT16 = _sort16_network()
_MERGE16 = _merge16_network()


def _exchange(x, net):
    for i, j in net:
        x[i], x[j] = jnp.maximum(x[i], x[j]), jnp.minimum(x[i], x[j])
    return x


def _top_values(s, top_ref):
    assert s.shape[0] == PEER_TOPK * SUBLANES
    x = _exchange([s[v * SUBLANES:(v + 1) * SUBLANES, :] for v in range(PEER_TOPK)], _SORT16)
    shift = SUBLANES // 2
    while shift >= 1:
        x = _exchange([jnp.maximum(x[v], pltpu.roll(x[PEER_TOPK - 1 - v], shift, 0)) for v in range(PEER_TOPK)],
                      _MERGE16)
        shift //= 2
    for r in range(PEER_TOPK):
        top_ref[r:r + 1, :] = x[r][0:1, :]


def _stair_candidates(v1, v2):
    half = PEER_TOPK // 2
    row8 = lax.broadcasted_iota(jnp.int32, (half, 1), 0)
    cands = [v1 + v2[0:1], v1[0:1] + v2[half:]]
    for bb in range(1, half):
        a_max = PEER_TOPK // (bb + 1) - 1
        c = v1[:half] + v2[bb:bb + 1]
        cands.append(jnp.where(row8 <= a_max, c, NEG_INF))
    return jnp.concatenate(cands, axis=0)


def _peer_sel_kernel(hp_ref, wq_ref, k1_ref, k2_ref, nb_ref, p1_ref, r2_ref, p2_ref, v1_s, v2_s, c_s):
    qp = _dot(wq_ref[...], hp_ref[0]).astype(MXU_DTYPE)
    for hd in range(PEER_HEADS):
        base = hd * PEER_D_KEY
        s1 = _dot(k1_ref[hd], qp[base:base + PEER_HALF])
        s2 = _dot(k2_ref[hd], qp[base + PEER_HALF:base + PEER_D_KEY])
        _top_values(s1, v1_s)
        _top_values(s2, v2_s)
        v1 = v1_s[...]
        v2 = v2_s[...]
        cand = _stair_candidates(v1, v2)
        pad = jnp.full((PEER_N_KEYS - cand.shape[0], cand.shape[1]), NEG_INF, jnp.float32)
        _top_values(jnp.concatenate([cand, pad], axis=0), c_s)
        t = c_s[PEER_TOPK - 1:PEER_TOPK, :]
        m1 = v1[0:1]
        m2 = v2[0:1]
        z = jnp.sum(jnp.where(cand >= t, jnp.exp(cand - (m1 + m2)), 0.0), axis=0, keepdims=True)
        nb_top = jnp.zeros_like(v1)
        for b in range(PEER_TOPK):
            nb_top = nb_top + jnp.where(v1 + v2[b:b + 1] >= t, 1.0, 0.0)
        nb = jnp.zeros_like(s1)
        r2 = jnp.full(s2.shape, float(PEER_TOPK), jnp.float32)
        for a in reversed(range(PEER_TOPK)):
            nb = jnp.where(s1 == v1[a:a + 1], nb_top[a:a + 1], nb)
            r2 = jnp.where(s2 == v2[a:a + 1], float(a), r2)
        nb_ref[0, hd] = _pair_words(nb)
        p1_ref[0, hd] = _pair_words(jnp.exp(s1 - m1) / z)
        r2_ref[0, hd] = r2.astype(r2_ref.dtype)
        p2_ref[0, hd] = jnp.exp(s2 - m2).astype(p2_ref.dtype)


def _pair_words(x):
    u = lax.bitcast_convert_type(x.astype(GATE_DTYPE).astype(jnp.float32), jnp.uint32)
    return u | (u >> 16)


def _peer_sel(hpT, wpqT, k1, k2, tm):
    b, d, s = hpT.shape
    shape = (b, PEER_HEADS, PEER_N_KEYS, s)
    words = jax.ShapeDtypeStruct(shape, jnp.uint32)
    sel = jax.ShapeDtypeStruct(shape, GATE_DTYPE)
    sel_spec = pl.BlockSpec((1, PEER_HEADS, PEER_N_KEYS, tm), lambda i, j: (i, 0, 0, j))
    return pl.pallas_call(
        _peer_sel_kernel,
        out_shape=(words, words, sel, sel),
        grid=(b, s // tm),
        in_specs=[pl.BlockSpec((1, d, tm), lambda i, j: (i, 0, j)), _const_spec(wpqT.shape),
                  _const_spec(k1.shape), _const_spec(k2.shape)],
        out_specs=(sel_spec, sel_spec, sel_spec, sel_spec),
        scratch_shapes=[pltpu.VMEM((PEER_TOPK, tm), jnp.float32) for _ in range(3)],
        compiler_params=_params(("parallel", "parallel"), "peer_sel"),
        name="peer_sel",
    )(hpT, wpqT, k1, k2)


def _peer_mix_kernel(hp_ref, ed_ref, euT_ref, nb_ref, p1_ref, r2_ref, p2_ref, x2_ref, gfin_ref, y_ref,
                     acc_ref, act_ref, w_ref, row_s, key_s):
    c = pl.program_id(2)
    n_c = pl.num_programs(2)
    tt = hp_ref.shape[2]
    n_i = ed_ref.shape[0] // PEER_N_KEYS
    jb = GATE_ROWS
    n_k = PEER_N_KEYS // jb

    @pl.when(c == 0)
    def _():
        acc_ref[...] = jnp.zeros_like(acc_ref)
        key_s[0] = r2_ref[0]
        key_s[1] = p2_ref[0]

    row_s[0] = nb_ref[0]
    row_s[1] = p1_ref[0]

    act_ref[...] = _gelu(_dot(ed_ref[...], hp_ref[0])).astype(act_ref.dtype)

    def row_bcast(words):
        return pltpu.bitcast(jnp.broadcast_to(words, (SUBLANES, LANES)), GATE_DTYPE)

    zero = jnp.zeros((jb, LANES), GATE_DTYPE)

    for lb in range(tt // LANES):
        ls = pl.ds(lb * LANES, LANES)
        for il in range(n_i):
            accs = [None] * n_k
            for hd in range(PEER_HEADS):
                bn = row_bcast(row_s[0, hd, il:il + 1, ls])
                bp = row_bcast(row_s[1, hd, il:il + 1, ls])
                for k in range(n_k):
                    r2v = key_s[0, hd, k * jb:(k + 1) * jb, ls]
                    p2v = key_s[1, hd, k * jb:(k + 1) * jb, ls]
                    term = bp * jnp.where(r2v < bn, p2v, zero)
                    accs[k] = term if accs[k] is None else accs[k] + term
            for k in range(n_k):
                rows = pl.ds(il * PEER_N_KEYS + k * jb, jb)
                w_ref[rows, ls] = (accs[k] * act_ref[rows, ls]).astype(w_ref.dtype)

    acc_ref[...] += _dot(euT_ref[0], w_ref[...])

    @pl.when(c == n_c - 1)
    def _():
        x3 = x2_ref[0] + acc_ref[...]
        y_ref[0] = _rmsnorm_cols(x3, gfin_ref[...]).T.astype(y_ref.dtype)


def _peer_mix(hpT, e_down, e_upT, nb, p1, r2, p2, x2T, gfin, tt):
    b, d, s = hpT.shape
    n_e = e_down.shape[0]
    ec = PEER_CHUNK
    n_i = ec // PEER_N_KEYS
    tok = lambda rows: pl.BlockSpec((1, rows, tt), lambda i, j, c: (i, 0, j))
    row_spec = pl.BlockSpec((1, PEER_HEADS, n_i, tt), lambda i, j, c: (i, 0, c, j))
    full_spec = pl.BlockSpec((1, PEER_HEADS, PEER_N_KEYS, tt), lambda i, j, c: (i, 0, 0, j))
    return pl.pallas_call(
        _peer_mix_kernel,
        out_shape=jax.ShapeDtypeStruct((b, s, d), jnp.float32),
        grid=(b, s // tt, n_e // ec),
        in_specs=[tok(d),
                  pl.BlockSpec((ec, d), lambda i, j, c: (c, 0)),
                  pl.BlockSpec((1, d, ec), lambda i, j, c: (c, 0, 0)),
                  row_spec, row_spec, full_spec, full_spec, tok(d), _const_spec((d, 1))],
        out_specs=pl.BlockSpec((1, tt, d), lambda i, j, c: (i, j, 0)),
        scratch_shapes=[pltpu.VMEM((d, tt), jnp.float32), pltpu.VMEM((ec, tt), GATE_DTYPE),
                        pltpu.VMEM((ec, tt), MXU_DTYPE),
                        pltpu.VMEM((2, PEER_HEADS, n_i, tt), jnp.uint32),
                        pltpu.VMEM((2, PEER_HEADS, PEER_N_KEYS, tt), GATE_DTYPE)],
        compiler_params=_params(("parallel", "parallel", "arbitrary"), "peer_mix"),
        name="peer_mix",
    )(hpT, e_down, e_upT, nb, p1, r2, p2, x2T, gfin)


def _rope_tables(s):
    rows = s // GRID_W
    row_pos = jnp.repeat(jnp.arange(rows, dtype=jnp.float32), GRID_W)
    col_pos = jnp.tile(jnp.arange(GRID_W, dtype=jnp.float32), rows)
    n_freq = HEAD_DIM // 4
    inv_freq = ROPE_THETA ** (-jnp.arange(n_freq, dtype=jnp.float32) / n_freq)
    ang_r = inv_freq[:, None] * row_pos[None, :]
    ang_c = inv_freq[:, None] * col_pos[None, :]
    cos_t = jnp.concatenate([jnp.cos(ang_r), jnp.cos(ang_r), jnp.cos(ang_c), jnp.cos(ang_c)], axis=0)
    sin_t = jnp.concatenate([-jnp.sin(ang_r), jnp.sin(ang_r), -jnp.sin(ang_c), jnp.sin(ang_c)], axis=0)
    return cos_t, sin_t


def _tiles(s):
    attn = max(SG_CHUNK, min(1024, ATTN_SCORE_BYTES // (2 * 2 * s)))
    return dict(inproj=min(512, s), attn=min(attn, s), merge=min(512, s), sel=min(256, s), mix=min(512, s))


def _layer(x, kmem, vmemT, w):
    s = x.shape[1]
    t = _tiles(s)
    cos_t, sin_t = _rope_tables(s)
    qT, k, vT, gattnT, sgpT = _inproj(x, w["gmix"], w["w_inT"], w["gq"], w["gk"], cos_t, sin_t, w["gsg"],
                                      w["wsT"], w["bs"], w["wsgoT"], t["inproj"])
    attnT = _attention(qT, k, vT, t["attn"])
    x2T, hpT = _merge(x, attnT, gattnT, sgpT, w["waoT"], w["woutT"], w["gxa"], w["wqxT"], kmem, vmemT,
                      w["woxT"], w["gffn"], t["merge"])
    nb, p1, r2, p2 = _peer_sel(hpT, w["wpqT"], w["k1"], w["k2"], t["sel"])
    return _peer_mix(hpT, w["e_down"], w["e_upT"], nb, p1, r2, p2, x2T, w["gfin"], t["mix"])


def kernel(x_prompt, x_sample, mem_prompt, mem_sample, norm_mix_g, w_in, q_norm_g, k_norm_g, sg_norm_g, sg_w,
           sg_b, w_attn_o, w_sg_o, w_out, norm_xa_g, norm_mem_g, wq_xa, wkv_xa, wo_xa, norm_ffn_g, w_peer_q,
           peer_k1, peer_k2, expert_down, expert_up, final_norm_g):
    assert w_in.shape[0] == 1, "single-layer trunk"
    w = _prep_weights(norm_mix_g[0], w_in[0], q_norm_g[0], k_norm_g[0], sg_norm_g[0], sg_w[0], sg_b[0],
                      w_attn_o[0], w_sg_o[0], w_out[0], norm_xa_g[0], norm_mem_g[0], wq_xa[0], wkv_xa[0],
                      wo_xa[0], norm_ffn_g[0], w_peer_q[0], peer_k1[0], peer_k2[0], expert_down[0],
                      expert_up[0], final_norm_g)
    outs = []
    for x, mem in ((x_prompt, mem_prompt), (x_sample, mem_sample)):
        kmem, vmemT = _kv_mem(mem, w["gmem"], w["wk"], w["wvT"])
        outs.append(_layer(x, kmem, vmemT, w))
    return tuple(outs)


def _prep_weights(norm_mix_g, w_in, q_norm_g, k_norm_g, sg_norm_g, sg_w, sg_b, w_attn_o, w_sg_o, w_out,
                  norm_xa_g, norm_mem_g, wq_xa, wkv_xa, wo_xa, norm_ffn_g, w_peer_q, peer_k1, peer_k2,
                  expert_down, expert_up, final_norm_g):
    cast_t = lambda a: a.astype(MXU_DTYPE).T
    col = lambda g: g.reshape(-1, 1).astype(jnp.float32)
    n_e, d = expert_up.shape
    e_upT = jnp.swapaxes(expert_up.astype(MXU_DTYPE).reshape(n_e // PEER_CHUNK, PEER_CHUNK, d), 1, 2)
    return dict(
        gmix=norm_mix_g.reshape(1, -1), w_inT=cast_t(w_in), gq=col(q_norm_g), gk=col(k_norm_g),
        gsg=col(sg_norm_g), wsT=jnp.swapaxes(sg_w, 1, 2).astype(MXU_DTYPE),
        bs=sg_b.reshape(SG_GROUPS, 1, SG_CHUNK), wsgoT=cast_t(w_sg_o),
        waoT=cast_t(w_attn_o), woutT=cast_t(w_out), gxa=col(norm_xa_g), wqxT=cast_t(wq_xa),
        woxT=cast_t(wo_xa), gffn=col(norm_ffn_g), wpqT=cast_t(w_peer_q),
        k1=peer_k1.astype(MXU_DTYPE), k2=peer_k2.astype(MXU_DTYPE),
        e_down=expert_down.astype(MXU_DTYPE), e_upT=e_upT, gfin=col(final_norm_g),
        gmem=norm_mem_g.reshape(1, -1), wk=wkv_xa[:, :D_MODEL].astype(MXU_DTYPE), wvT=cast_t(wkv_xa[:, D_MODEL:]),
    )
```

```python
import math

import jax
import jax.numpy as jnp
from jax import lax
from jax.experimental import pallas as pl
from jax.experimental.pallas import tpu as pltpu

D_MODEL = 1024
GRID_W = 64
N_HEADS = 16
N_KV_HEADS = 4
HEAD_DIM = 64
KV_GROUP = N_HEADS // N_KV_HEADS
ATTN_Q_WIDTH = N_HEADS * HEAD_DIM
ATTN_KV_WIDTH = N_KV_HEADS * HEAD_DIM
ROPE_THETA = 10000.0
SG_WIDTH = 1024
SG_GROUPS = 8
SG_GROUP_DIM = SG_WIDTH // SG_GROUPS
SG_CHUNK = 128
XA_HEADS = 4
XA_HEAD_DIM = D_MODEL // XA_HEADS
PEER_HEADS = 8
PEER_N_KEYS = 128
PEER_D_KEY = 256
PEER_HALF = PEER_D_KEY // 2
PEER_TOPK = 16
PEER_CHUNK = 1024
EPS = 1e-6

_Q0 = 0
_K0 = _Q0 + ATTN_Q_WIDTH
_V0 = _K0 + ATTN_KV_WIDTH
_Z0 = _V0 + ATTN_KV_WIDTH
_G0 = _Z0 + 2 * SG_WIDTH
_IN_WIDTH = _G0 + 2 * D_MODEL

LANES = 128
SUBLANES = 8
V7X_VMEM_BYTES = 64 * 1024 * 1024
MIB = 1024 * 1024
ATTN_SCORE_BYTES = 16 * MIB
VMEM_LIMIT_MIB = dict(kv_mem=32, inproj=56, attn=48, merge=48, peer_sel=48, peer_mix=56)
assert max(VMEM_LIMIT_MIB.values()) * MIB < V7X_VMEM_BYTES

MXU_DTYPE = jnp.bfloat16
GATE_DTYPE = jnp.bfloat16
GATE_ROWS = 2 * SUBLANES
NEG_INF = float("-inf")


def _dot(a, b):
    return jnp.dot(a, b, preferred_element_type=jnp.float32)


def _dot_nt(a, b):
    return lax.dot_general(a, b, (((1,), (1,)), ((), ())), preferred_element_type=jnp.float32)


def _gelu(x):
    c = math.sqrt(2.0 / math.pi)
    return 0.5 * x * (1.0 + jnp.tanh(c * (x + 0.044715 * (x * x * x))))


def _sigmoid(x):
    return 1.0 / (1.0 + jnp.exp(-x))


def _rmsnorm_rows(x, g_row):
    ms = jnp.mean(x * x, axis=-1, keepdims=True)
    return x * lax.rsqrt(ms + EPS) * g_row


def _rmsnorm_cols(xT, g_col):
    ms = jnp.mean(xT * xT, axis=0, keepdims=True)
    return xT * lax.rsqrt(ms + EPS) * g_col


def _params(semantics, stage):
    return pltpu.CompilerParams(dimension_semantics=semantics, vmem_limit_bytes=VMEM_LIMIT_MIB[stage] * MIB)


def _const_spec(shape):
    nd = len(shape)
    return pl.BlockSpec(shape, lambda *_: (0,) * nd)


def _kv_mem_kernel(mem_ref, g_ref, wk_ref, wvT_ref, k_ref, vT_ref):
    mn = _rmsnorm_rows(mem_ref[0], g_ref[...]).astype(MXU_DTYPE)
    k_ref[0] = _dot(mn, wk_ref[...]).astype(k_ref.dtype)
    vT_ref[0] = _dot_nt(wvT_ref[...], mn).astype(vT_ref.dtype)


def _kv_mem(mem, g_row, wk, wvT):
    nb, m, d = mem.shape
    return pl.pallas_call(
        _kv_mem_kernel,
        out_shape=(jax.ShapeDtypeStruct((nb, m, d), MXU_DTYPE), jax.ShapeDtypeStruct((nb, d, m), MXU_DTYPE)),
        grid=(nb,),
        in_specs=[pl.BlockSpec((1, m, d), lambda b: (b, 0, 0)), _const_spec((1, d)),
                  _const_spec((d, d)), _const_spec((d, d))],
        out_specs=(pl.BlockSpec((1, m, d), lambda b: (b, 0, 0)), pl.BlockSpec((1, d, m), lambda b: (b, 0, 0))),
        compiler_params=_params(("parallel",), "kv_mem"),
        name="kv_mem",
    )(mem, g_row, wk, wvT)


def _head_norm_rope(t, g_col, cos, sin):
    ms = jnp.mean(t * t, axis=0, keepdims=True)
    t = t * lax.rsqrt(ms + EPS) * g_col
    q4 = HEAD_DIM // 4
    sw = jnp.concatenate([t[q4:2 * q4], t[0:q4], t[3 * q4:], t[2 * q4:3 * q4]], axis=0)
    return t * cos + sw * sin


def _inproj_kernel(x_ref, gmix_ref, w_ref, gq_ref, gk_ref, cos_ref, sin_ref, gsg_ref, wsT_ref, bs_ref, wsgo_ref,
                   qT_ref, k_ref, vT_ref, gattn_ref, sgp_ref):
    tm = x_ref.shape[1]
    h = _rmsnorm_rows(x_ref[0], gmix_ref[...]).astype(MXU_DTYPE)

    def proj_t(lo, hi):
        return _dot_nt(w_ref[lo:hi, :], h)

    cos = cos_ref[...]
    sin = sin_ref[...]

    q_t = proj_t(_Q0, _K0)
    scale = math.log2(math.e) / math.sqrt(HEAD_DIM)
    for hd in range(N_HEADS):
        r = _head_norm_rope(q_t[hd * HEAD_DIM:(hd + 1) * HEAD_DIM], gq_ref[...], cos, sin) * scale
        qT_ref[0, hd * HEAD_DIM:(hd + 1) * HEAD_DIM, :] = r.astype(qT_ref.dtype)

    k_t = proj_t(_K0, _V0)
    k_rot = jnp.concatenate(
        [_head_norm_rope(k_t[g * HEAD_DIM:(g + 1) * HEAD_DIM], gk_ref[...], cos, sin) for g in range(N_KV_HEADS)],
        axis=0)
    k_tok = k_rot.T
    for g in range(N_KV_HEADS):
        k_ref[0, g] = k_tok[:, g * HEAD_DIM:(g + 1) * HEAD_DIM].astype(k_ref.dtype)

    vT_ref[0] = proj_t(_V0, _Z0).astype(vT_ref.dtype)

    z_t = _gelu(proj_t(_Z0, _G0))
    u = z_t[:SG_WIDTH]
    vn = _rmsnorm_cols(z_t[SG_WIDTH:], gsg_ref[...]).astype(MXU_DTYPE)
    n_chunks = tm // SG_CHUNK
    sv_groups = []
    for g in range(SG_GROUPS):
        vg = vn[g * SG_GROUP_DIM:(g + 1) * SG_GROUP_DIM]
        lhs = jnp.concatenate([vg[:, c * SG_CHUNK:(c + 1) * SG_CHUNK] for c in range(n_chunks)], axis=0)
        r = _dot(lhs, wsT_ref[g]) + bs_ref[g]
        sv_groups.append(jnp.concatenate(
            [r[c * SG_GROUP_DIM:(c + 1) * SG_GROUP_DIM] for c in range(n_chunks)], axis=1))
    sg = (u * jnp.concatenate(sv_groups, axis=0)).astype(MXU_DTYPE)
    sg_branch = _dot(wsgo_ref[...], sg)

    gates = _sigmoid(proj_t(_G0, _IN_WIDTH))
    gattn_ref[0] = gates[:D_MODEL].astype(gattn_ref.dtype)
    sgp_ref[0] = (gates[D_MODEL:] * sg_branch).astype(sgp_ref.dtype)


def _inproj(x, gmix, w_inT, gq, gk, cos_t, sin_t, gsg, wsT, bs, wsgoT, tm):
    b, s, d = x.shape
    grid = (b, s // tm)
    tok = lambda rows: pl.BlockSpec((1, rows, tm), lambda i, j: (i, 0, j))
    return pl.pallas_call(
        _inproj_kernel,
        out_shape=(jax.ShapeDtypeStruct((b, ATTN_Q_WIDTH, s), MXU_DTYPE),
                   jax.ShapeDtypeStruct((b, N_KV_HEADS, s, HEAD_DIM), MXU_DTYPE),
                   jax.ShapeDtypeStruct((b, ATTN_KV_WIDTH, s), MXU_DTYPE),
                   jax.ShapeDtypeStruct((b, D_MODEL, s), MXU_DTYPE),
                   jax.ShapeDtypeStruct((b, D_MODEL, s), jnp.float32)),
        grid=grid,
        in_specs=[pl.BlockSpec((1, tm, d), lambda i, j: (i, j, 0)),
                  _const_spec((1, d)), _const_spec(w_inT.shape),
                  _const_spec((HEAD_DIM, 1)), _const_spec((HEAD_DIM, 1)),
                  pl.BlockSpec((HEAD_DIM, tm), lambda i, j: (0, j)),
                  pl.BlockSpec((HEAD_DIM, tm), lambda i, j: (0, j)),
                  _const_spec((SG_WIDTH, 1)), _const_spec(wsT.shape), _const_spec(bs.shape),
                  _const_spec(wsgoT.shape)],
        out_specs=(tok(ATTN_Q_WIDTH),
                   pl.BlockSpec((1, N_KV_HEADS, tm, HEAD_DIM), lambda i, j: (i, 0, j, 0)),
                   tok(ATTN_KV_WIDTH), tok(D_MODEL), tok(D_MODEL)),
        compiler_params=_params(("parallel", "parallel"), "inproj"),
        name="inproj",
    )(x, gmix, w_inT, gq, gk, cos_t, sin_t, gsg, wsT, bs, wsgoT)


def _attn_kernel(qT_ref, k_ref, vT_ref, o_ref):
    tq = qT_ref.shape[2]
    k = k_ref[0, 0]
    vT = vT_ref[0]
    vT1 = jnp.concatenate([vT, jnp.ones((GATE_ROWS, vT.shape[1]), vT.dtype)], axis=0)
    for pair in range(KV_GROUP // 2):
        heads = (2 * pair, 2 * pair + 1)
        qT2 = jnp.concatenate([qT_ref[0, a * HEAD_DIM:(a + 1) * HEAD_DIM, :] for a in heads], axis=1)
        sT = _dot(k, qT2).astype(MXU_DTYPE)
        m = jnp.max(sT, axis=0, keepdims=True)
        p = jnp.exp2(sT - m)
        o = _dot(vT1, p)
        oT = o[:HEAD_DIM] / o[HEAD_DIM:HEAD_DIM + 1]
        for i, a in enumerate(heads):
            o_ref[0, a * HEAD_DIM:(a + 1) * HEAD_DIM, :] = oT[:, i * tq:(i + 1) * tq].astype(o_ref.dtype)


def _attention(qT, k, vT, tq):
    b, _, s = qT.shape
    rows = KV_GROUP * HEAD_DIM
    return pl.pallas_call(
        _attn_kernel,
        out_shape=jax.ShapeDtypeStruct((b, ATTN_Q_WIDTH, s), MXU_DTYPE),
        grid=(b, N_KV_HEADS, s // tq),
        in_specs=[pl.BlockSpec((1, rows, tq), lambda i, g, j: (i, g, j)),
                  pl.BlockSpec((1, 1, s, HEAD_DIM), lambda i, g, j: (i, g, 0, 0)),
                  pl.BlockSpec((1, HEAD_DIM, s), lambda i, g, j: (i, g, 0))],
        out_specs=pl.BlockSpec((1, rows, tq), lambda i, g, j: (i, g, j)),
        compiler_params=_params(("parallel", "parallel", "parallel"), "attn"),
        name="attn",
    )(qT, k, vT)


def _merge_kernel(x_ref, attn_ref, gattn_ref, sgp_ref, wao_ref, wout_ref, gxa_ref, wq_ref, km_ref, vmT_ref,
                  wo_ref, gffn_ref, x2_ref, hp_ref):
    xT = x_ref[0].T
    attn_branch = _dot(wao_ref[...], attn_ref[0])
    mix = gattn_ref[0].astype(jnp.float32) * attn_branch + sgp_ref[0]
    x1 = xT + _dot(wout_ref[...], mix.astype(MXU_DTYPE))

    hx = _rmsnorm_cols(x1, gxa_ref[...]).astype(MXU_DTYPE)
    qx = (_dot(wq_ref[...], hx) * (1.0 / math.sqrt(XA_HEAD_DIM))).astype(MXU_DTYPE)
    outs = []
    for hd in range(XA_HEADS):
        lo, hi = hd * XA_HEAD_DIM, (hd + 1) * XA_HEAD_DIM
        sT = _dot(km_ref[0, :, lo:hi], qx[lo:hi])
        m = jnp.max(sT, axis=0, keepdims=True)
        p = jnp.exp(sT - m)
        l = jnp.sum(p, axis=0, keepdims=True)
        outs.append((_dot(vmT_ref[0, lo:hi, :], p.astype(MXU_DTYPE)) / l).astype(MXU_DTYPE))
    x2 = x1 + _dot(wo_ref[...], jnp.concatenate(outs, axis=0))
    x2_ref[0] = x2
    hp_ref[0] = _rmsnorm_cols(x2, gffn_ref[...]).astype(hp_ref.dtype)


def _merge(x, attnT, gattnT, sgpT, waoT, woutT, gxa, wqT, kmem, vmemT, woT, gffn, tm):
    b, s, d = x.shape
    m = kmem.shape[1]
    tok = pl.BlockSpec((1, d, tm), lambda i, j: (i, 0, j))
    wspec = _const_spec((d, d))
    col = _const_spec((d, 1))
    return pl.pallas_call(
        _merge_kernel,
        out_shape=(jax.ShapeDtypeStruct((b, d, s), jnp.float32), jax.ShapeDtypeStruct((b, d, s), MXU_DTYPE)),
        grid=(b, s // tm),
        in_specs=[pl.BlockSpec((1, tm, d), lambda i, j: (i, j, 0)), tok, tok, tok, wspec, wspec, col, wspec,
                  pl.BlockSpec((1, m, d), lambda i, j: (i, 0, 0)), pl.BlockSpec((1, d, m), lambda i, j: (i, 0, 0)),
                  wspec, col],
        out_specs=(tok, tok),
        compiler_params=_params(("parallel", "parallel"), "merge"),
        name="merge",
    )(x, attnT, gattnT, sgpT, waoT, woutT, gxa, wqT, kmem, vmemT, woT, gffn)


def _sort16_network():
    def merge(lo, hi, r):
        step = 2 * r
        if step < hi - lo:
            yield from merge(lo, hi, step)
            yield from merge(lo + r, hi, step)
            yield from ((i, i + r) for i in range(lo + r, hi - r, step))
        else:
            yield (lo, lo + r)

    def sort(lo, hi):
        if hi > lo:
            mid = lo + (hi - lo) // 2
            yield from sort(lo, mid)
            yield from sort(mid + 1, hi)
            yield from merge(lo, hi, 1)

    return tuple(sort(0, PEER_TOPK - 1))


def _merge16_network():
    out, k = [], PEER_TOPK // 2
    while k >= 1:
        out += [(i, i + k) for i in range(PEER_TOPK) if i % (2 * k) < k]
        k //= 2
    return tuple(out)


_SORT16 = _sort16_network()
_MERGE16 = _merge16_network()


def _exchange(x, net):
    for i, j in net:
        x[i], x[j] = jnp.maximum(x[i], x[j]), jnp.minimum(x[i], x[j])
    return x


def _top_values(s, top_ref):
    assert s.shape[0] == PEER_TOPK * SUBLANES
    x = _exchange([s[v * SUBLANES:(v + 1) * SUBLANES, :] for v in range(PEER_TOPK)], _SORT16)
    shift = SUBLANES // 2
    while shift >= 1:
        x = _exchange([jnp.maximum(x[v], pltpu.roll(x[PEER_TOPK - 1 - v], shift, 0)) for v in range(PEER_TOPK)],
                      _MERGE16)
        shift //= 2
    for r in range(PEER_TOPK):
        top_ref[r:r + 1, :] = x[r][0:1, :]


def _stair_candidates(v1, v2):
    half = PEER_TOPK // 2
    row8 = lax.broadcasted_iota(jnp.int32, (half, 1), 0)
    cands = [v1 + v2[0:1], v1[0:1] + v2[half:]]
    for bb in range(1, half):
        a_max = PEER_TOPK // (bb + 1) - 1
        c = v1[:half] + v2[bb:bb + 1]
        cands.append(jnp.where(row8 <= a_max, c, NEG_INF))
    return jnp.concatenate(cands, axis=0)


def _peer_sel_kernel(hp_ref, wq_ref, k1_ref, k2_ref, nb_ref, p1_ref, r2_ref, p2_ref, v1_s, v2_s, c_s):
    qp = _dot(wq_ref[...], hp_ref[0]).astype(MXU_DTYPE)
    for hd in range(PEER_HEADS):
        base = hd * PEER_D_KEY
        s1 = _dot(k1_ref[hd], qp[base:base + PEER_HALF])
        s2 = _dot(k2_ref[hd], qp[base + PEER_HALF:base + PEER_D_KEY])
        _top_values(s1, v1_s)
        _top_values(s2, v2_s)
        v1 = v1_s[...]
        v2 = v2_s[...]
        cand = _stair_candidates(v1, v2)
        pad = jnp.full((PEER_N_KEYS - cand.shape[0], cand.shape[1]), NEG_INF, jnp.float32)
        _top_values(jnp.concatenate([cand, pad], axis=0), c_s)
        t = c_s[PEER_TOPK - 1:PEER_TOPK, :]
        m1 = v1[0:1]
        m2 = v2[0:1]
        z = jnp.sum(jnp.where(cand >= t, jnp.exp(cand - (m1 + m2)), 0.0), axis=0, keepdims=True)
        nb_top = jnp.zeros_like(v1)
        for b in range(PEER_TOPK):
            nb_top = nb_top + jnp.where(v1 + v2[b:b + 1] >= t, 1.0, 0.0)
        nb = jnp.zeros_like(s1)
        r2 = jnp.full(s2.shape, float(PEER_TOPK), jnp.float32)
        for a in reversed(range(PEER_TOPK)):
            nb = jnp.where(s1 == v1[a:a + 1], nb_top[a:a + 1], nb)
            r2 = jnp.where(s2 == v2[a:a + 1], float(a), r2)
        nb_ref[0, hd] = _pair_words(nb)
        p1_ref[0, hd] = _pair_words(jnp.exp(s1 - m1) / z)
        r2_ref[0, hd] = r2.astype(r2_ref.dtype)
        p2_ref[0, hd] = jnp.exp(s2 - m2).astype(p2_ref.dtype)


def _pair_words(x):
    u = lax.bitcast_convert_type(x.astype(GATE_DTYPE).astype(jnp.float32), jnp.uint32)
    return u | (u >> 16)


def _peer_sel(hpT, wpqT, k1, k2, tm):
    b, d, s = hpT.shape
    shape = (b, PEER_HEADS, PEER_N_KEYS, s)
    words = jax.ShapeDtypeStruct(shape, jnp.uint32)
    sel = jax.ShapeDtypeStruct(shape, GATE_DTYPE)
    sel_spec = pl.BlockSpec((1, PEER_HEADS, PEER_N_KEYS, tm), lambda i, j: (i, 0, 0, j))
    return pl.pallas_call(
        _peer_sel_kernel,
        out_shape=(words, words, sel, sel),
        grid=(b, s // tm),
        in_specs=[pl.BlockSpec((1, d, tm), lambda i, j: (i, 0, j)), _const_spec(wpqT.shape),
                  _const_spec(k1.shape), _const_spec(k2.shape)],
        out_specs=(sel_spec, sel_spec, sel_spec, sel_spec),
        scratch_shapes=[pltpu.VMEM((PEER_TOPK, tm), jnp.float32) for _ in range(3)],
        compiler_params=_params(("parallel", "parallel"), "peer_sel"),
        name="peer_sel",
    )(hpT, wpqT, k1, k2)


def _peer_mix_kernel(hp_ref, ed_ref, euT_ref, nb_ref, p1_ref, r2_ref, p2_ref, x2_ref, gfin_ref, y_ref,
                     acc_ref, act_ref, w_ref, row_s, key_s):
    c = pl.program_id(2)
    n_c = pl.num_programs(2)
    tt = hp_ref.shape[2]
    n_i = ed_ref.shape[0] // PEER_N_KEYS
    jb = GATE_ROWS
    n_k = PEER_N_KEYS // jb

    @pl.when(c == 0)
    def _():
        acc_ref[...] = jnp.zeros_like(acc_ref)
        key_s[0] = r2_ref[0]
        key_s[1] = p2_ref[0]

    row_s[0] = nb_ref[0]
    row_s[1] = p1_ref[0]

    act_ref[...] = _gelu(_dot(ed_ref[...], hp_ref[0])).astype(act_ref.dtype)

    def row_bcast(words):
        return pltpu.bitcast(jnp.broadcast_to(words, (SUBLANES, LANES)), GATE_DTYPE)

    zero = jnp.zeros((jb, LANES), GATE_DTYPE)

    for lb in range(tt // LANES):
        ls = pl.ds(lb * LANES, LANES)
        for il in range(n_i):
            accs = [None] * n_k
            for hd in range(PEER_HEADS):
                bn = row_bcast(row_s[0, hd, il:il + 1, ls])
                bp = row_bcast(row_s[1, hd, il:il + 1, ls])
                for k in range(n_k):
                    r2v = key_s[0, hd, k * jb:(k + 1) * jb, ls]
                    p2v = key_s[1, hd, k * jb:(k + 1) * jb, ls]
                    term = bp * jnp.where(r2v < bn, p2v, zero)
                    accs[k] = term if accs[k] is None else accs[k] + term
            for k in range(n_k):
                rows = pl.ds(il * PEER_N_KEYS + k * jb, jb)
                w_ref[rows, ls] = (accs[k] * act_ref[rows, ls]).astype(w_ref.dtype)

    acc_ref[...] += _dot(euT_ref[0], w_ref[...])

    @pl.when(c == n_c - 1)
    def _():
        x3 = x2_ref[0] + acc_ref[...]
        y_ref[0] = _rmsnorm_cols(x3, gfin_ref[...]).T.astype(y_ref.dtype)


def _peer_mix(hpT, e_down, e_upT, nb, p1, r2, p2, x2T, gfin, tt):
    b, d, s = hpT.shape
    n_e = e_down.shape[0]
    ec = PEER_CHUNK
    n_i = ec // PEER_N_KEYS
    tok = lambda rows: pl.BlockSpec((1, rows, tt), lambda i, j, c: (i, 0, j))
    row_spec = pl.BlockSpec((1, PEER_HEADS, n_i, tt), lambda i, j, c: (i, 0, c, j))
    full_spec = pl.BlockSpec((1, PEER_HEADS, PEER_N_KEYS, tt), lambda i, j, c: (i, 0, 0, j))
    return pl.pallas_call(
        _peer_mix_kernel,
        out_shape=jax.ShapeDtypeStruct((b, s, d), jnp.float32),
        grid=(b, s // tt, n_e // ec),
        in_specs=[tok(d),
                  pl.BlockSpec((ec, d), lambda i, j, c: (c, 0)),
                  pl.BlockSpec((1, d, ec), lambda i, j, c: (c, 0, 0)),
                  row_spec, row_spec, full_spec, full_spec, tok(d), _const_spec((d, 1))],
        out_specs=pl.BlockSpec((1, tt, d), lambda i, j, c: (i, j, 0)),
        scratch_shapes=[pltpu.VMEM((d, tt), jnp.float32), pltpu.VMEM((ec, tt), GATE_DTYPE),
                        pltpu.VMEM((ec, tt), MXU_DTYPE),
                        pltpu.VMEM((2, PEER_HEADS, n_i, tt), jnp.uint32),
                        pltpu.VMEM((2, PEER_HEADS, PEER_N_KEYS, tt), GATE_DTYPE)],
        compiler_params=_params(("parallel", "parallel", "arbitrary"), "peer_mix"),
        name="peer_mix",
    )(hpT, e_down, e_upT, nb, p1, r2, p2, x2T, gfin)


def _rope_tables(s):
    rows = s // GRID_W
    row_pos = jnp.repeat(jnp.arange(rows, dtype=jnp.float32), GRID_W)
    col_pos = jnp.tile(jnp.arange(GRID_W, dtype=jnp.float32), rows)
    n_freq = HEAD_DIM // 4
    inv_freq = ROPE_THETA ** (-jnp.arange(n_freq, dtype=jnp.float32) / n_freq)
    ang_r = inv_freq[:, None] * row_pos[None, :]
    ang_c = inv_freq[:, None] * col_pos[None, :]
    cos_t = jnp.concatenate([jnp.cos(ang_r), jnp.cos(ang_r), jnp.cos(ang_c), jnp.cos(ang_c)], axis=0)
    sin_t = jnp.concatenate([-jnp.sin(ang_r), jnp.sin(ang_r), -jnp.sin(ang_c), jnp.sin(ang_c)], axis=0)
    return cos_t, sin_t


def _tiles(s):
    attn = max(SG_CHUNK, min(1024, ATTN_SCORE_BYTES // (2 * 2 * s)))
    return dict(inproj=min(512, s), attn=min(attn, s), merge=min(512, s), sel=min(256, s), mix=min(512, s))


def _layer(x, kmem, vmemT, w):
    s = x.shape[1]
    t = _tiles(s)
    cos_t, sin_t = _rope_tables(s)
    qT, k, vT, gattnT, sgpT = _inproj(x, w["gmix"], w["w_inT"], w["gq"], w["gk"], cos_t, sin_t, w["gsg"],
                                      w["wsT"], w["bs"], w["wsgoT"], t["inproj"])
    attnT = _attention(qT, k, vT, t["attn"])
    x2T, hpT = _merge(x, attnT, gattnT, sgpT, w["waoT"], w["woutT"], w["gxa"], w["wqxT"], kmem, vmemT,
                      w["woxT"], w["gffn"], t["merge"])
    nb, p1, r2, p2 = _peer_sel(hpT, w["wpqT"], w["k1"], w["k2"], t["sel"])
    return _peer_mix(hpT, w["e_down"], w["e_upT"], nb, p1, r2, p2, x2T, w["gfin"], t["mix"])


def kernel(x_prompt, x_sample, mem_prompt, mem_sample, norm_mix_g, w_in, q_norm_g, k_norm_g, sg_norm_g, sg_w,
           sg_b, w_attn_o, w_sg_o, w_out, norm_xa_g, norm_mem_g, wq_xa, wkv_xa, wo_xa, norm_ffn_g, w_peer_q,
           peer_k1, peer_k2, expert_down, expert_up, final_norm_g):
    assert w_in.shape[0] == 1, "single-layer trunk"
    w = _prep_weights(norm_mix_g[0], w_in[0], q_norm_g[0], k_norm_g[0], sg_norm_g[0], sg_w[0], sg_b[0],
                      w_attn_o[0], w_sg_o[0], w_out[0], norm_xa_g[0], norm_mem_g[0], wq_xa[0], wkv_xa[0],
                      wo_xa[0], norm_ffn_g[0], w_peer_q[0], peer_k1[0], peer_k2[0], expert_down[0],
                      expert_up[0], final_norm_g)
    outs = []
    for x, mem in ((x_prompt, mem_prompt), (x_sample, mem_sample)):
        kmem, vmemT = _kv_mem(mem, w["gmem"], w["wk"], w["wvT"])
        outs.append(_layer(x, kmem, vmemT, w))
    return tuple(outs)


def _prep_weights(norm_mix_g, w_in, q_norm_g, k_norm_g, sg_norm_g, sg_w, sg_b, w_attn_o, w_sg_o, w_out,
                  norm_xa_g, norm_mem_g, wq_xa, wkv_xa, wo_xa, norm_ffn_g, w_peer_q, peer_k1, peer_k2,
                  expert_down, expert_up, final_norm_g):
    cast_t = lambda a: a.astype(MXU_DTYPE).T
    col = lambda g: g.reshape(-1, 1).astype(jnp.float32)
    n_e, d = expert_up.shape
    e_upT = jnp.swapaxes(expert_up.astype(MXU_DTYPE).reshape(n_e // PEER_CHUNK, PEER_CHUNK, d), 1, 2)
    return dict(
        gmix=norm_mix_g.reshape(1, -1), w_inT=cast_t(w_in), gq=col(q_norm_g), gk=col(k_norm_g),
        gsg=col(sg_norm_g), wsT=jnp.swapaxes(sg_w, 1, 2).astype(MXU_DTYPE),
        bs=sg_b.reshape(SG_GROUPS, 1, SG_CHUNK), wsgoT=cast_t(w_sg_o),
        waoT=cast_t(w_attn_o), woutT=cast_t(w_out), gxa=col(norm_xa_g), wqxT=cast_t(wq_xa),
        woxT=cast_t(wo_xa), gffn=col(norm_ffn_g), wpqT=cast_t(w_peer_q),
        k1=peer_k1.astype(MXU_DTYPE), k2=peer_k2.astype(MXU_DTYPE),
        e_down=expert_down.astype(MXU_DTYPE), e_upT=e_upT, gfin=col(final_norm_g),
        gmem=norm_mem_g.reshape(1, -1), wk=wkv_xa[:, :D_MODEL].astype(MXU_DTYPE), wvT=cast_t(wkv_xa[:, D_MODEL:]),
    )
```

```python
import math

import jax
import jax.numpy as jnp
from jax import lax
from jax.experimental import pallas as pl
from jax.experimental.pallas import tpu as pltpu

D_MODEL = 1024
GRID_W = 64
N_HEADS = 16
N_KV_HEADS = 4
HEAD_DIM = 64
KV_GROUP = N_HEADS // N_KV_HEADS
ATTN_Q_WIDTH = N_HEADS * HEAD_DIM
ATTN_KV_WIDTH = N_KV_HEADS * HEAD_DIM
ROPE_THETA = 10000.0
SG_WIDTH = 1024
SG_GROUPS = 8
SG_GROUP_DIM = SG_WIDTH // SG_GROUPS
SG_CHUNK = 128
XA_HEADS = 4
XA_HEAD_DIM = D_MODEL // XA_HEADS
PEER_HEADS = 8
PEER_N_KEYS = 128
PEER_D_KEY = 256
PEER_HALF = PEER_D_KEY // 2
PEER_TOPK = 16
PEER_CHUNK = 1024
EPS = 1e-6

_Q0 = 0
_K0 = _Q0 + ATTN_Q_WIDTH
_V0 = _K0 + ATTN_KV_WIDTH
_Z0 = _V0 + ATTN_KV_WIDTH
_G0 = _Z0 + 2 * SG_WIDTH
_IN_WIDTH = _G0 + 2 * D_MODEL

LANES = 128
SUBLANES = 8
V7X_VMEM_BYTES = 64 * 1024 * 1024
MIB = 1024 * 1024
ATTN_SCORE_BYTES = 16 * MIB
VMEM_LIMIT_MIB = dict(kv_mem=32, inproj=56, attn=48, merge=48, peer_sel=48, peer_mix=56)
assert max(VMEM_LIMIT_MIB.values()) * MIB < V7X_VMEM_BYTES

MXU_DTYPE = jnp.bfloat16
GATE_DTYPE = jnp.bfloat16
GATE_ROWS = 2 * SUBLANES
NEG_INF = float("-inf")


def _dot(a, b):
    return jnp.dot(a, b, preferred_element_type=jnp.float32)


def _dot_nt(a, b):
    return lax.dot_general(a, b, (((1,), (1,)), ((), ())), preferred_element_type=jnp.float32)


def _gelu(x):
    c = math.sqrt(2.0 / math.pi)
    return 0.5 * x * (1.0 + jnp.tanh(c * (x + 0.044715 * (x * x * x))))


def _sigmoid(x):
    return 1.0 / (1.0 + jnp.exp(-x))


def _rmsnorm_rows(x, g_row):
    ms = jnp.mean(x * x, axis=-1, keepdims=True)
    return x * lax.rsqrt(ms + EPS) * g_row


def _rmsnorm_cols(xT, g_col):
    ms = jnp.mean(xT * xT, axis=0, keepdims=True)
    return xT * lax.rsqrt(ms + EPS) * g_col


def _params(semantics, stage):
    return pltpu.CompilerParams(dimension_semantics=semantics, vmem_limit_bytes=VMEM_LIMIT_MIB[stage] * MIB)


def _const_spec(shape):
    nd = len(shape)
    return pl.BlockSpec(shape, lambda *_: (0,) * nd)


def _kv_mem_kernel(mem_ref, g_ref, wk_ref, wvT_ref, k_ref, vT_ref):
    mn = _rmsnorm_rows(mem_ref[0], g_ref[...]).astype(MXU_DTYPE)
    k_ref[0] = _dot(mn, wk_ref[...]).astype(k_ref.dtype)
    vT_ref[0] = _dot_nt(wvT_ref[...], mn).astype(vT_ref.dtype)


def _kv_mem(mem, g_row, wk, wvT):
    nb, m, d = mem.shape
    return pl.pallas_call(
        _kv_mem_kernel,
        out_shape=(jax.ShapeDtypeStruct((nb, m, d), MXU_DTYPE), jax.ShapeDtypeStruct((nb, d, m), MXU_DTYPE)),
        grid=(nb,),
        in_specs=[pl.BlockSpec((1, m, d), lambda b: (b, 0, 0)), _const_spec((1, d)),
                  _const_spec((d, d)), _const_spec((d, d))],
        out_specs=(pl.BlockSpec((1, m, d), lambda b: (b, 0, 0)), pl.BlockSpec((1, d, m), lambda b: (b, 0, 0))),
        compiler_params=_params(("parallel",), "kv_mem"),
        name="kv_mem",
    )(mem, g_row, wk, wvT)


def _head_norm_rope(t, g_col, cos, sin):
    ms = jnp.mean(t * t, axis=0, keepdims=True)
    t = t * lax.rsqrt(ms + EPS) * g_col
    q4 = HEAD_DIM // 4
    sw = jnp.concatenate([t[q4:2 * q4], t[0:q4], t[3 * q4:], t[2 * q4:3 * q4]], axis=0)
    return t * cos + sw * sin


def _inproj_kernel(x_ref, gmix_ref, w_ref, gq_ref, gk_ref, cos_ref, sin_ref, gsg_ref, wsT_ref, bs_ref, wsgo_ref,
                   qT_ref, k_ref, vT_ref, gattn_ref, sgp_ref):
    tm = x_ref.shape[1]
    h = _rmsnorm_rows(x_ref[0], gmix_ref[...]).astype(MXU_DTYPE)

    def proj_t(lo, hi):
        return _dot_nt(w_ref[lo:hi, :], h)

    cos = cos_ref[...]
    sin = sin_ref[...]

    q_t = proj_t(_Q0, _K0)
    scale = math.log2(math.e) / math.sqrt(HEAD_DIM)
    for hd in range(N_HEADS):
        r = _head_norm_rope(q_t[hd * HEAD_DIM:(hd + 1) * HEAD_DIM], gq_ref[...], cos, sin) * scale
        qT_ref[0, hd * HEAD_DIM:(hd + 1) * HEAD_DIM, :] = r.astype(qT_ref.dtype)

    k_t = proj_t(_K0, _V0)
    k_rot = jnp.concatenate(
        [_head_norm_rope(k_t[g * HEAD_DIM:(g + 1) * HEAD_DIM], gk_ref[...], cos, sin) for g in range(N_KV_HEADS)],
        axis=0)
    k_tok = k_rot.T
    for g in range(N_KV_HEADS):
        k_ref[0, g] = k_tok[:, g * HEAD_DIM:(g + 1) * HEAD_DIM].astype(k_ref.dtype)

    vT_ref[0] = proj_t(_V0, _Z0).astype(vT_ref.dtype)

    z_t = _gelu(proj_t(_Z0, _G0))
    u = z_t[:SG_WIDTH]
    vn = _rmsnorm_cols(z_t[SG_WIDTH:], gsg_ref[...]).astype(MXU_DTYPE)
    n_chunks = tm // SG_CHUNK
    sv_groups = []
    for g in range(SG_GROUPS):
        vg = vn[g * SG_GROUP_DIM:(g + 1) * SG_GROUP_DIM]
        lhs = jnp.concatenate([vg[:, c * SG_CHUNK:(c + 1) * SG_CHUNK] for c in range(n_chunks)], axis=0)
        r = _dot(lhs, wsT_ref[g]) + bs_ref[g]
        sv_groups.append(jnp.concatenate(
            [r[c * SG_GROUP_DIM:(c + 1) * SG_GROUP_DIM] for c in range(n_chunks)], axis=1))
    sg = (u * jnp.concatenate(sv_groups, axis=0)).astype(MXU_DTYPE)
    sg_branch = _dot(wsgo_ref[...], sg)

    gates = _sigmoid(proj_t(_G0, _IN_WIDTH))
    gattn_ref[0] = gates[:D_MODEL].astype(gattn_ref.dtype)
    sgp_ref[0] = (gates[D_MODEL:] * sg_branch).astype(sgp_ref.dtype)


def _inproj(x, gmix, w_inT, gq, gk, cos_t, sin_t, gsg, wsT, bs, wsgoT, tm):
    b, s, d = x.shape
    grid = (b, s // tm)
    tok = lambda rows: pl.BlockSpec((1, rows, tm), lambda i, j: (i, 0, j))
    return pl.pallas_call(
        _inproj_kernel,
        out_shape=(jax.ShapeDtypeStruct((b, ATTN_Q_WIDTH, s), MXU_DTYPE),
                   jax.ShapeDtypeStruct((b, N_KV_HEADS, s, HEAD_DIM), MXU_DTYPE),
                   jax.ShapeDtypeStruct((b, ATTN_KV_WIDTH, s), MXU_DTYPE),
                   jax.ShapeDtypeStruct((b, D_MODEL, s), MXU_DTYPE),
                   jax.ShapeDtypeStruct((b, D_MODEL, s), jnp.float32)),
        grid=grid,
        in_specs=[pl.BlockSpec((1, tm, d), lambda i, j: (i, j, 0)),
                  _const_spec((1, d)), _const_spec(w_inT.shape),
                  _const_spec((HEAD_DIM, 1)), _const_spec((HEAD_DIM, 1)),
                  pl.BlockSpec((HEAD_DIM, tm), lambda i, j: (0, j)),
                  pl.BlockSpec((HEAD_DIM, tm), lambda i, j: (0, j)),
                  _const_spec((SG_WIDTH, 1)), _const_spec(wsT.shape), _const_spec(bs.shape),
                  _const_spec(wsgoT.shape)],
        out_specs=(tok(ATTN_Q_WIDTH),
                   pl.BlockSpec((1, N_KV_HEADS, tm, HEAD_DIM), lambda i, j: (i, 0, j, 0)),
                   tok(ATTN_KV_WIDTH), tok(D_MODEL), tok(D_MODEL)),
        compiler_params=_params(("parallel", "parallel"), "inproj"),
        name="inproj",
    )(x, gmix, w_inT, gq, gk, cos_t, sin_t, gsg, wsT, bs, wsgoT)


def _attn_kernel(qT_ref, k_ref, vT_ref, o_ref):
    tq = qT_ref.shape[2]
    k = k_ref[0, 0]
    vT = vT_ref[0]
    vT1 = jnp.concatenate([vT, jnp.ones((GATE_ROWS, vT.shape[1]), vT.dtype)], axis=0)
    for pair in range(KV_GROUP // 2):
        heads = (2 * pair, 2 * pair + 1)
        qT2 = jnp.concatenate([qT_ref[0, a * HEAD_DIM:(a + 1) * HEAD_DIM, :] for a in heads], axis=1)
        sT = _dot(k, qT2).astype(MXU_DTYPE)
        m = jnp.max(sT, axis=0, keepdims=True)
        p = jnp.exp2(sT - m)
        o = _dot(vT1, p)
        oT = o[:HEAD_DIM] / o[HEAD_DIM:HEAD_DIM + 1]
        for i, a in enumerate(heads):
            o_ref[0, a * HEAD_DIM:(a + 1) * HEAD_DIM, :] = oT[:, i * tq:(i + 1) * tq].astype(o_ref.dtype)


def _attention(qT, k, vT, tq):
    b, _, s = qT.shape
    rows = KV_GROUP * HEAD_DIM
    return pl.pallas_call(
        _attn_kernel,
        out_shape=jax.ShapeDtypeStruct((b, ATTN_Q_WIDTH, s), MXU_DTYPE),
        grid=(b, N_KV_HEADS, s // tq),
        in_specs=[pl.BlockSpec((1, rows, tq), lambda i, g, j: (i, g, j)),
                  pl.BlockSpec((1, 1, s, HEAD_DIM), lambda i, g, j: (i, g, 0, 0)),
                  pl.BlockSpec((1, HEAD_DIM, s), lambda i, g, j: (i, g, 0))],
        out_specs=pl.BlockSpec((1, rows, tq), lambda i, g, j: (i, g, j)),
        compiler_params=_params(("parallel", "parallel", "parallel"), "attn"),
        name="attn",
    )(qT, k, vT)


def _merge_kernel(x_ref, attn_ref, gattn_ref, sgp_ref, wao_ref, wout_ref, gxa_ref, wq_ref, km_ref, vmT_ref,
                  wo_ref, gffn_ref, x2_ref, hp_ref):
    xT = x_ref[0].T
    attn_branch = _dot(wao_ref[...], attn_ref[0])
    mix = gattn_ref[0].astype(jnp.float32) * attn_branch + sgp_ref[0]
    x1 = xT + _dot(wout_ref[...], mix.astype(MXU_DTYPE))

    hx = _rmsnorm_cols(x1, gxa_ref[...]).astype(MXU_DTYPE)
    qx = (_dot(wq_ref[...], hx) * (1.0 / math.sqrt(XA_HEAD_DIM))).astype(MXU_DTYPE)
    outs = []
    for hd in range(XA_HEADS):
        lo, hi = hd * XA_HEAD_DIM, (hd + 1) * XA_HEAD_DIM
        sT = _dot(km_ref[0, :, lo:hi], qx[lo:hi])
        m = jnp.max(sT, axis=0, keepdims=True)
        p = jnp.exp(sT - m)
        l = jnp.sum(p, axis=0, keepdims=True)
        outs.append((_dot(vmT_ref[0, lo:hi, :], p.astype(MXU_DTYPE)) / l).astype(MXU_DTYPE))
    x2 = x1 + _dot(wo_ref[...], jnp.concatenate(outs, axis=0))
    x2_ref[0] = x2
    hp_ref[0] = _rmsnorm_cols(x2, gffn_ref[...]).astype(hp_ref.dtype)


def _merge(x, attnT, gattnT, sgpT, waoT, woutT, gxa, wqT, kmem, vmemT, woT, gffn, tm):
    b, s, d = x.shape
    m = kmem.shape[1]
    tok = pl.BlockSpec((1, d, tm), lambda i, j: (i, 0, j))
    wspec = _const_spec((d, d))
    col = _const_spec((d, 1))
    return pl.pallas_call(
        _merge_kernel,
        out_shape=(jax.ShapeDtypeStruct((b, d, s), jnp.float32), jax.ShapeDtypeStruct((b, d, s), MXU_DTYPE)),
        grid=(b, s // tm),
        in_specs=[pl.BlockSpec((1, tm, d), lambda i, j: (i, j, 0)), tok, tok, tok, wspec, wspec, col, wspec,
                  pl.BlockSpec((1, m, d), lambda i, j: (i, 0, 0)), pl.BlockSpec((1, d, m), lambda i, j: (i, 0, 0)),
                  wspec, col],
        out_specs=(tok, tok),
        compiler_params=_params(("parallel", "parallel"), "merge"),
        name="merge",
    )(x, attnT, gattnT, sgpT, waoT, woutT, gxa, wqT, kmem, vmemT, woT, gffn)


def _sort16_network():
    def merge(lo, hi, r):
        step = 2 * r
        if step < hi - lo:
            yield from merge(lo, hi, step)
            yield from merge(lo + r, hi, step)
            yield from ((i, i + r) for i in range(lo + r, hi - r, step))
        else:
            yield (lo, lo + r)

    def sort(lo, hi):
        if hi > lo:
            mid = lo + (hi - lo) // 2
            yield from sort(lo, mid)
            yield from sort(mid + 1, hi)
            yield from merge(lo, hi, 1)

    return tuple(sort(0, PEER_TOPK - 1))


def _merge16_network():
    out, k = [], PEER_TOPK // 2
    while k >= 1:
        out += [(i, i + k) for i in range(PEER_TOPK) if i % (2 * k) < k]
        k //= 2
    return tuple(out)


_SORT16 = _sort16_network()
_MERGE16 = _merge16_network()


def _exchange(x, net):
    for i, j in net:
        x[i], x[j] = jnp.maximum(x[i], x[j]), jnp.minimum(x[i], x[j])
    return x


def _top_values(s, top_ref):
    assert s.shape[0] == PEER_TOPK * SUBLANES
    x = _exchange([s[v * SUBLANES:(v + 1) * SUBLANES, :] for v in range(PEER_TOPK)], _SORT16)
    shift = SUBLANES // 2
    while shift >= 1:
        x = _exchange([jnp.maximum(x[v], pltpu.roll(x[PEER_TOPK - 1 - v], shift, 0)) for v in range(PEER_TOPK)],
                      _MERGE16)
        shift //= 2
    for r in range(PEER_TOPK):
        top_ref[r:r + 1, :] = x[r][0:1, :]


def _stair_candidates(v1, v2):
    half = PEER_TOPK // 2
    row8 = lax.broadcasted_iota(jnp.int32, (half, 1), 0)
    cands = [v1 + v2[0:1], v1[0:1] + v2[half:]]
    for bb in range(1, half):
        a_max = PEER_TOPK // (bb + 1) - 1
        c = v1[:half] + v2[bb:bb + 1]
        cands.append(jnp.where(row8 <= a_max, c, NEG_INF))
    return jnp.concatenate(cands, axis=0)


def _peer_sel_kernel(hp_ref, wq_ref, k1_ref, k2_ref, nb_ref, p1_ref, r2_ref, p2_ref, v1_s, v2_s, c_s):
    qp = _dot(wq_ref[...], hp_ref[0]).astype(MXU_DTYPE)
    for hd in range(PEER_HEADS):
        base = hd * PEER_D_KEY
        s1 = _dot(k1_ref[hd], qp[base:base + PEER_HALF])
        s2 = _dot(k2_ref[hd], qp[base + PEER_HALF:base + PEER_D_KEY])
        _top_values(s1, v1_s)
        _top_values(s2, v2_s)
        v1 = v1_s[...]
        v2 = v2_s[...]
        cand = _stair_candidates(v1, v2)
        pad = jnp.full((PEER_N_KEYS - cand.shape[0], cand.shape[1]), NEG_INF, jnp.float32)
        _top_values(jnp.concatenate([cand, pad], axis=0), c_s)
        t = c_s[PEER_TOPK - 1:PEER_TOPK, :]
        m1 = v1[0:1]
        m2 = v2[0:1]
        z = jnp.sum(jnp.where(cand >= t, jnp.exp(cand - (m1 + m2)), 0.0), axis=0, keepdims=True)
        nb_top = jnp.zeros_like(v1)
        for b in range(PEER_TOPK):
            nb_top = nb_top + jnp.where(v1 + v2[b:b + 1] >= t, 1.0, 0.0)
        nb = jnp.zeros_like(s1)
        r2 = jnp.full(s2.shape, float(PEER_TOPK), jnp.float32)
        for a in reversed(range(PEER_TOPK)):
            nb = jnp.where(s1 == v1[a:a + 1], nb_top[a:a + 1], nb)
            r2 = jnp.where(s2 == v2[a:a + 1], float(a), r2)
        nb_ref[0, hd] = _pair_words(nb)
        p1_ref[0, hd] = _pair_words(jnp.exp(s1 - m1) / z)
        r2_ref[0, hd] = r2.astype(r2_ref.dtype)
        p2_ref[0, hd] = jnp.exp(s2 - m2).astype(p2_ref.dtype)


def _pair_words(x):
    u = lax.bitcast_convert_type(x.astype(GATE_DTYPE).astype(jnp.float32), jnp.uint32)
    return u | (u >> 16)


def _peer_sel(hpT, wpqT, k1, k2, tm):
    b, d, s = hpT.shape
    shape = (b, PEER_HEADS, PEER_N_KEYS, s)
    words = jax.ShapeDtypeStruct(shape, jnp.uint32)
    sel = jax.ShapeDtypeStruct(shape, GATE_DTYPE)
    sel_spec = pl.BlockSpec((1, PEER_HEADS, PEER_N_KEYS, tm), lambda i, j: (i, 0, 0, j))
    return pl.pallas_call(
        _peer_sel_kernel,
        out_shape=(words, words, sel, sel),
        grid=(b, s // tm),
        in_specs=[pl.BlockSpec((1, d, tm), lambda i, j: (i, 0, j)), _const_spec(wpqT.shape),
                  _const_spec(k1.shape), _const_spec(k2.shape)],
        out_specs=(sel_spec, sel_spec, sel_spec, sel_spec),
        scratch_shapes=[pltpu.VMEM((PEER_TOPK, tm), jnp.float32) for _ in range(3)],
        compiler_params=_params(("parallel", "parallel"), "peer_sel"),
        name="peer_sel",
    )(hpT, wpqT, k1, k2)


def _peer_mix_kernel(hp_ref, ed_ref, eu_lag_ref, eu_last_ref, nb_ref, p1_ref, r2_ref, p2_ref, x2_ref, gfin_ref,
                     y_ref, acc_ref, g_ref, w_ref, row_s, key_s):
    c = pl.program_id(2)
    n_c = pl.num_programs(2)
    tt = hp_ref.shape[2]
    n_i = ed_ref.shape[0] // PEER_N_KEYS
    jb = GATE_ROWS
    n_k = PEER_N_KEYS // jb

    @pl.when(c == 0)
    def _():
        acc_ref[...] = jnp.zeros_like(acc_ref)
        w_ref[...] = jnp.zeros_like(w_ref)
        key_s[0] = r2_ref[0]
        key_s[1] = p2_ref[0]

    row_s[0] = nb_ref[0]
    row_s[1] = p1_ref[0]

    def row_bcast(words):
        return pltpu.bitcast(jnp.broadcast_to(words, (SUBLANES, LANES)), GATE_DTYPE)

    zero = jnp.zeros((jb, LANES), GATE_DTYPE)

    for lb in range(tt // LANES):
        ls = pl.ds(lb * LANES, LANES)
        for il in range(n_i):
            accs = [None] * n_k
            for hd in range(PEER_HEADS):
                bn = row_bcast(row_s[0, hd, il:il + 1, ls])
                bp = row_bcast(row_s[1, hd, il:il + 1, ls])
                for k in range(n_k):
                    r2v = key_s[0, hd, k * jb:(k + 1) * jb, ls]
                    p2v = key_s[1, hd, k * jb:(k + 1) * jb, ls]
                    term = bp * jnp.where(r2v < bn, p2v, zero)
                    accs[k] = term if accs[k] is None else accs[k] + term
            for k in range(n_k):
                g_ref[pl.ds(il * PEER_N_KEYS + k * jb, jb), ls] = accs[k]

    acc_ref[...] += _dot(eu_lag_ref[0], w_ref[...])
    act = _gelu(_dot(ed_ref[...], hp_ref[0]))
    w_ref[...] = (g_ref[...] * act.astype(GATE_DTYPE)).astype(w_ref.dtype)

    @pl.when(c == n_c - 1)
    def _():
        x3 = x2_ref[0] + acc_ref[...] + _dot(eu_last_ref[0], w_ref[...])
        y_ref[0] = _rmsnorm_cols(x3, gfin_ref[...]).T.astype(y_ref.dtype)


def _peer_mix(hpT, e_down, e_upT, nb, p1, r2, p2, x2T, gfin, tt):
    b, d, s = hpT.shape
    n_e = e_down.shape[0]
    ec = PEER_CHUNK
    n_i = ec // PEER_N_KEYS
    tok = lambda rows: pl.BlockSpec((1, rows, tt), lambda i, j, c: (i, 0, j))
    row_spec = pl.BlockSpec((1, PEER_HEADS, n_i, tt), lambda i, j, c: (i, 0, c, j))
    full_spec = pl.BlockSpec((1, PEER_HEADS, PEER_N_KEYS, tt), lambda i, j, c: (i, 0, 0, j))
    return pl.pallas_call(
        _peer_mix_kernel,
        out_shape=jax.ShapeDtypeStruct((b, s, d), jnp.float32),
        grid=(b, s // tt, n_e // ec),
        in_specs=[tok(d),
                  pl.BlockSpec((ec, d), lambda i, j, c: (c, 0)),
                  pl.BlockSpec((1, d, ec), lambda i, j, c: (jnp.maximum(c - 1, 0), 0, 0)),
                  pl.BlockSpec((1, d, ec), lambda i, j, c: (n_e // ec - 1, 0, 0)),
                  row_spec, row_spec, full_spec, full_spec, tok(d), _const_spec((d, 1))],
        out_specs=pl.BlockSpec((1, tt, d), lambda i, j, c: (i, j, 0)),
        scratch_shapes=[pltpu.VMEM((d, tt), jnp.float32), pltpu.VMEM((ec, tt), GATE_DTYPE),
                        pltpu.VMEM((ec, tt), MXU_DTYPE),
                        pltpu.VMEM((2, PEER_HEADS, n_i, tt), jnp.uint32),
                        pltpu.VMEM((2, PEER_HEADS, PEER_N_KEYS, tt), GATE_DTYPE)],
        compiler_params=_params(("parallel", "parallel", "arbitrary"), "peer_mix"),
        name="peer_mix",
    )(hpT, e_down, e_upT, e_upT, nb, p1, r2, p2, x2T, gfin)


def _rope_tables(s):
    rows = s // GRID_W
    row_pos = jnp.repeat(jnp.arange(rows, dtype=jnp.float32), GRID_W)
    col_pos = jnp.tile(jnp.arange(GRID_W, dtype=jnp.float32), rows)
    n_freq = HEAD_DIM // 4
    inv_freq = ROPE_THETA ** (-jnp.arange(n_freq, dtype=jnp.float32) / n_freq)
    ang_r = inv_freq[:, None] * row_pos[None, :]
    ang_c = inv_freq[:, None] * col_pos[None, :]
    cos_t = jnp.concatenate([jnp.cos(ang_r), jnp.cos(ang_r), jnp.cos(ang_c), jnp.cos(ang_c)], axis=0)
    sin_t = jnp.concatenate([-jnp.sin(ang_r), jnp.sin(ang_r), -jnp.sin(ang_c), jnp.sin(ang_c)], axis=0)
    return cos_t, sin_t


def _tiles(s):
    attn = max(SG_CHUNK, min(1024, ATTN_SCORE_BYTES // (2 * 2 * s)))
    return dict(inproj=min(512, s), attn=min(attn, s), merge=min(512, s), sel=min(256, s), mix=min(512, s))


def _layer(x, kmem, vmemT, w):
    s = x.shape[1]
    t = _tiles(s)
    cos_t, sin_t = _rope_tables(s)
    qT, k, vT, gattnT, sgpT = _inproj(x, w["gmix"], w["w_inT"], w["gq"], w["gk"], cos_t, sin_t, w["gsg"],
                                      w["wsT"], w["bs"], w["wsgoT"], t["inproj"])
    attnT = _attention(qT, k, vT, t["attn"])
    x2T, hpT = _merge(x, attnT, gattnT, sgpT, w["waoT"], w["woutT"], w["gxa"], w["wqxT"], kmem, vmemT,
                      w["woxT"], w["gffn"], t["merge"])
    nb, p1, r2, p2 = _peer_sel(hpT, w["wpqT"], w["k1"], w["k2"], t["sel"])
    return _peer_mix(hpT, w["e_down"], w["e_upT"], nb, p1, r2, p2, x2T, w["gfin"], t["mix"])


def kernel(x_prompt, x_sample, mem_prompt, mem_sample, norm_mix_g, w_in, q_norm_g, k_norm_g, sg_norm_g, sg_w,
           sg_b, w_attn_o, w_sg_o, w_out, norm_xa_g, norm_mem_g, wq_xa, wkv_xa, wo_xa, norm_ffn_g, w_peer_q,
           peer_k1, peer_k2, expert_down, expert_up, final_norm_g):
    assert w_in.shape[0] == 1, "single-layer trunk"
    w = _prep_weights(norm_mix_g[0], w_in[0], q_norm_g[0], k_norm_g[0], sg_norm_g[0], sg_w[0], sg_b[0],
                      w_attn_o[0], w_sg_o[0], w_out[0], norm_xa_g[0], norm_mem_g[0], wq_xa[0], wkv_xa[0],
                      wo_xa[0], norm_ffn_g[0], w_peer_q[0], peer_k1[0], peer_k2[0], expert_down[0],
                      expert_up[0], final_norm_g)
    outs = []
    for x, mem in ((x_prompt, mem_prompt), (x_sample, mem_sample)):
        kmem, vmemT = _kv_mem(mem, w["gmem"], w["wk"], w["wvT"])
        outs.append(_layer(x, kmem, vmemT, w))
    return tuple(outs)


def _prep_weights(norm_mix_g, w_in, q_norm_g, k_norm_g, sg_norm_g, sg_w, sg_b, w_attn_o, w_sg_o, w_out,
                  norm_xa_g, norm_mem_g, wq_xa, wkv_xa, wo_xa, norm_ffn_g, w_peer_q, peer_k1, peer_k2,
                  expert_down, expert_up, final_norm_g):
    cast_t = lambda a: a.astype(MXU_DTYPE).T
    col = lambda g: g.reshape(-1, 1).astype(jnp.float32)
    n_e, d = expert_up.shape
    e_upT = jnp.swapaxes(expert_up.astype(MXU_DTYPE).reshape(n_e // PEER_CHUNK, PEER_CHUNK, d), 1, 2)
    return dict(
        gmix=norm_mix_g.reshape(1, -1), w_inT=cast_t(w_in), gq=col(q_norm_g), gk=col(k_norm_g),
        gsg=col(sg_norm_g), wsT=jnp.swapaxes(sg_w, 1, 2).astype(MXU_DTYPE),
        bs=sg_b.reshape(SG_GROUPS, 1, SG_CHUNK), wsgoT=cast_t(w_sg_o),
        waoT=cast_t(w_attn_o), woutT=cast_t(w_out), gxa=col(norm_xa_g), wqxT=cast_t(wq_xa),
        woxT=cast_t(wo_xa), gffn=col(norm_ffn_g), wpqT=cast_t(w_peer_q),
        k1=peer_k1.astype(MXU_DTYPE), k2=peer_k2.astype(MXU_DTYPE),
        e_down=expert_down.astype(MXU_DTYPE), e_upT=e_upT, gfin=col(final_norm_g),
        gmem=norm_mem_g.reshape(1, -1), wk=wkv_xa[:, :D_MODEL].astype(MXU_DTYPE), wvT=cast_t(wkv_xa[:, D_MODEL:]),
    )
```

```python
import math

import jax
import jax.numpy as jnp
from jax import lax
from jax.experimental import pallas as pl
from jax.experimental.pallas import tpu as pltpu

D_MODEL = 1024
GRID_W = 64
N_HEADS = 16
N_KV_HEADS = 4
HEAD_DIM = 64
KV_GROUP = N_HEADS // N_KV_HEADS
ATTN_Q_WIDTH = N_HEADS * HEAD_DIM
ATTN_KV_WIDTH = N_KV_HEADS * HEAD_DIM
ROPE_THETA = 10000.0
SG_WIDTH = 1024
SG_GROUPS = 8
SG_GROUP_DIM = SG_WIDTH // SG_GROUPS
SG_CHUNK = 128
XA_HEADS = 4
XA_HEAD_DIM = D_MODEL // XA_HEADS
PEER_HEADS = 8
PEER_N_KEYS = 128
PEER_D_KEY = 256
PEER_HALF = PEER_D_KEY // 2
PEER_TOPK = 16
PEER_CHUNK = 2048
EPS = 1e-6

_Q0 = 0
_K0 = _Q0 + ATTN_Q_WIDTH
_V0 = _K0 + ATTN_KV_WIDTH
_Z0 = _V0 + ATTN_KV_WIDTH
_G0 = _Z0 + 2 * SG_WIDTH
_IN_WIDTH = _G0 + 2 * D_MODEL

LANES = 128
SUBLANES = 8
V7X_VMEM_BYTES = 64 * 1024 * 1024
MIB = 1024 * 1024
ATTN_SCORE_BYTES = 16 * MIB
VMEM_LIMIT_MIB = dict(kv_mem=32, inproj=56, attn=48, merge=48, peer_sel=48, peer_mix=56)
assert max(VMEM_LIMIT_MIB.values()) * MIB < V7X_VMEM_BYTES

MXU_DTYPE = jnp.bfloat16
GATE_DTYPE = jnp.bfloat16
GATE_ROWS = 2 * SUBLANES
NEG_INF = float("-inf")


def _dot(a, b):
    return jnp.dot(a, b, preferred_element_type=jnp.float32)


def _dot_nt(a, b):
    return lax.dot_general(a, b, (((1,), (1,)), ((), ())), preferred_element_type=jnp.float32)


def _gelu(x):
    c = math.sqrt(2.0 / math.pi)
    return 0.5 * x * (1.0 + jnp.tanh(c * (x + 0.044715 * (x * x * x))))


def _sigmoid(x):
    return 1.0 / (1.0 + jnp.exp(-x))


def _rmsnorm_rows(x, g_row):
    ms = jnp.mean(x * x, axis=-1, keepdims=True)
    return x * lax.rsqrt(ms + EPS) * g_row


def _rmsnorm_cols(xT, g_col):
    ms = jnp.mean(xT * xT, axis=0, keepdims=True)
    return xT * lax.rsqrt(ms + EPS) * g_col


def _params(semantics, stage):
    return pltpu.CompilerParams(dimension_semantics=semantics, vmem_limit_bytes=VMEM_LIMIT_MIB[stage] * MIB)


def _const_spec(shape):
    nd = len(shape)
    return pl.BlockSpec(shape, lambda *_: (0,) * nd)


def _kv_mem_kernel(mem_ref, g_ref, wk_ref, wvT_ref, k_ref, vT_ref):
    mn = _rmsnorm_rows(mem_ref[0], g_ref[...]).astype(MXU_DTYPE)
    k_ref[0] = _dot(mn, wk_ref[...]).astype(k_ref.dtype)
    vT_ref[0] = _dot_nt(wvT_ref[...], mn).astype(vT_ref.dtype)


def _kv_mem(mem, g_row, wk, wvT):
    nb, m, d = mem.shape
    return pl.pallas_call(
        _kv_mem_kernel,
        out_shape=(jax.ShapeDtypeStruct((nb, m, d), MXU_DTYPE), jax.ShapeDtypeStruct((nb, d, m), MXU_DTYPE)),
        grid=(nb,),
        in_specs=[pl.BlockSpec((1, m, d), lambda b: (b, 0, 0)), _const_spec((1, d)),
                  _const_spec((d, d)), _const_spec((d, d))],
        out_specs=(pl.BlockSpec((1, m, d), lambda b: (b, 0, 0)), pl.BlockSpec((1, d, m), lambda b: (b, 0, 0))),
        compiler_params=_params(("parallel",), "kv_mem"),
        name="kv_mem",
    )(mem, g_row, wk, wvT)


def _head_norm_rope(t, g_col, cos, sin):
    ms = jnp.mean(t * t, axis=0, keepdims=True)
    t = t * lax.rsqrt(ms + EPS) * g_col
    q4 = HEAD_DIM // 4
    sw = jnp.concatenate([t[q4:2 * q4], t[0:q4], t[3 * q4:], t[2 * q4:3 * q4]], axis=0)
    return t * cos + sw * sin


def _inproj_kernel(x_ref, gmix_ref, w_ref, gq_ref, gk_ref, cos_ref, sin_ref, gsg_ref, wsT_ref, bs_ref, wsgo_ref,
                   qT_ref, k_ref, vT_ref, gattn_ref, sgp_ref):
    tm = x_ref.shape[1]
    h = _rmsnorm_rows(x_ref[0], gmix_ref[...]).astype(MXU_DTYPE)

    def proj_t(lo, hi):
        return _dot_nt(w_ref[lo:hi, :], h)

    cos = cos_ref[...]
    sin = sin_ref[...]

    q_t = proj_t(_Q0, _K0)
    scale = math.log2(math.e) / math.sqrt(HEAD_DIM)
    for hd in range(N_HEADS):
        r = _head_norm_rope(q_t[hd * HEAD_DIM:(hd + 1) * HEAD_DIM], gq_ref[...], cos, sin) * scale
        qT_ref[0, hd * HEAD_DIM:(hd + 1) * HEAD_DIM, :] = r.astype(qT_ref.dtype)

    k_t = proj_t(_K0, _V0)
    k_rot = jnp.concatenate(
        [_head_norm_rope(k_t[g * HEAD_DIM:(g + 1) * HEAD_DIM], gk_ref[...], cos, sin) for g in range(N_KV_HEADS)],
        axis=0)
    k_tok = k_rot.T
    for g in range(N_KV_HEADS):
        k_ref[0, g] = k_tok[:, g * HEAD_DIM:(g + 1) * HEAD_DIM].astype(k_ref.dtype)

    vT_ref[0] = proj_t(_V0, _Z0).astype(vT_ref.dtype)

    z_t = _gelu(proj_t(_Z0, _G0))
    u = z_t[:SG_WIDTH]
    vn = _rmsnorm_cols(z_t[SG_WIDTH:], gsg_ref[...]).astype(MXU_DTYPE)
    n_chunks = tm // SG_CHUNK
    sv_groups = []
    for g in range(SG_GROUPS):
        vg = vn[g * SG_GROUP_DIM:(g + 1) * SG_GROUP_DIM]
        lhs = jnp.concatenate([vg[:, c * SG_CHUNK:(c + 1) * SG_CHUNK] for c in range(n_chunks)], axis=0)
        r = _dot(lhs, wsT_ref[g]) + bs_ref[g]
        sv_groups.append(jnp.concatenate(
            [r[c * SG_GROUP_DIM:(c + 1) * SG_GROUP_DIM] for c in range(n_chunks)], axis=1))
    sg = (u * jnp.concatenate(sv_groups, axis=0)).astype(MXU_DTYPE)
    sg_branch = _dot(wsgo_ref[...], sg)

    gates = _sigmoid(proj_t(_G0, _IN_WIDTH))
    gattn_ref[0] = gates[:D_MODEL].astype(gattn_ref.dtype)
    sgp_ref[0] = (gates[D_MODEL:] * sg_branch).astype(sgp_ref.dtype)


def _inproj(x, gmix, w_inT, gq, gk, cos_t, sin_t, gsg, wsT, bs, wsgoT, tm):
    b, s, d = x.shape
    grid = (b, s // tm)
    tok = lambda rows: pl.BlockSpec((1, rows, tm), lambda i, j: (i, 0, j))
    return pl.pallas_call(
        _inproj_kernel,
        out_shape=(jax.ShapeDtypeStruct((b, ATTN_Q_WIDTH, s), MXU_DTYPE),
                   jax.ShapeDtypeStruct((b, N_KV_HEADS, s, HEAD_DIM), MXU_DTYPE),
                   jax.ShapeDtypeStruct((b, ATTN_KV_WIDTH, s), MXU_DTYPE),
                   jax.ShapeDtypeStruct((b, D_MODEL, s), MXU_DTYPE),
                   jax.ShapeDtypeStruct((b, D_MODEL, s), jnp.float32)),
        grid=grid,
        in_specs=[pl.BlockSpec((1, tm, d), lambda i, j: (i, j, 0)),
                  _const_spec((1, d)), _const_spec(w_inT.shape),
                  _const_spec((HEAD_DIM, 1)), _const_spec((HEAD_DIM, 1)),
                  pl.BlockSpec((HEAD_DIM, tm), lambda i, j: (0, j)),
                  pl.BlockSpec((HEAD_DIM, tm), lambda i, j: (0, j)),
                  _const_spec((SG_WIDTH, 1)), _const_spec(wsT.shape), _const_spec(bs.shape),
                  _const_spec(wsgoT.shape)],
        out_specs=(tok(ATTN_Q_WIDTH),
                   pl.BlockSpec((1, N_KV_HEADS, tm, HEAD_DIM), lambda i, j: (i, 0, j, 0)),
                   tok(ATTN_KV_WIDTH), tok(D_MODEL), tok(D_MODEL)),
        compiler_params=_params(("parallel", "parallel"), "inproj"),
        name="inproj",
    )(x, gmix, w_inT, gq, gk, cos_t, sin_t, gsg, wsT, bs, wsgoT)


def _attn_kernel(qT_ref, k_ref, vT_ref, o_ref):
    tq = qT_ref.shape[2]
    k = k_ref[0, 0]
    vT = vT_ref[0]
    vT1 = jnp.concatenate([vT, jnp.ones((GATE_ROWS, vT.shape[1]), vT.dtype)], axis=0)
    for pair in range(KV_GROUP // 2):
        heads = (2 * pair, 2 * pair + 1)
        qT2 = jnp.concatenate([qT_ref[0, a * HEAD_DIM:(a + 1) * HEAD_DIM, :] for a in heads], axis=1)
        sT = _dot(k, qT2).astype(MXU_DTYPE)
        m = jnp.max(sT, axis=0, keepdims=True)
        p = jnp.exp2(sT - m)
        o = _dot(vT1, p)
        oT = o[:HEAD_DIM] / o[HEAD_DIM:HEAD_DIM + 1]
        for i, a in enumerate(heads):
            o_ref[0, a * HEAD_DIM:(a + 1) * HEAD_DIM, :] = oT[:, i * tq:(i + 1) * tq].astype(o_ref.dtype)


def _attention(qT, k, vT, tq):
    b, _, s = qT.shape
    rows = KV_GROUP * HEAD_DIM
    return pl.pallas_call(
        _attn_kernel,
        out_shape=jax.ShapeDtypeStruct((b, ATTN_Q_WIDTH, s), MXU_DTYPE),
        grid=(b, N_KV_HEADS, s // tq),
        in_specs=[pl.BlockSpec((1, rows, tq), lambda i, g, j: (i, g, j)),
                  pl.BlockSpec((1, 1, s, HEAD_DIM), lambda i, g, j: (i, g, 0, 0)),
                  pl.BlockSpec((1, HEAD_DIM, s), lambda i, g, j: (i, g, 0))],
        out_specs=pl.BlockSpec((1, rows, tq), lambda i, g, j: (i, g, j)),
        compiler_params=_params(("parallel", "parallel", "parallel"), "attn"),
        name="attn",
    )(qT, k, vT)


def _merge_kernel(x_ref, attn_ref, gattn_ref, sgp_ref, wao_ref, wout_ref, gxa_ref, wq_ref, km_ref, vmT_ref,
                  wo_ref, gffn_ref, x2_ref, hp_ref):
    xT = x_ref[0].T
    attn_branch = _dot(wao_ref[...], attn_ref[0])
    mix = gattn_ref[0].astype(jnp.float32) * attn_branch + sgp_ref[0]
    x1 = xT + _dot(wout_ref[...], mix.astype(MXU_DTYPE))

    hx = _rmsnorm_cols(x1, gxa_ref[...]).astype(MXU_DTYPE)
    qx = (_dot(wq_ref[...], hx) * (1.0 / math.sqrt(XA_HEAD_DIM))).astype(MXU_DTYPE)
    outs = []
    for hd in range(XA_HEADS):
        lo, hi = hd * XA_HEAD_DIM, (hd + 1) * XA_HEAD_DIM
        sT = _dot(km_ref[0, :, lo:hi], qx[lo:hi])
        m = jnp.max(sT, axis=0, keepdims=True)
        p = jnp.exp(sT - m)
        l = jnp.sum(p, axis=0, keepdims=True)
        outs.append((_dot(vmT_ref[0, lo:hi, :], p.astype(MXU_DTYPE)) / l).astype(MXU_DTYPE))
    x2 = x1 + _dot(wo_ref[...], jnp.concatenate(outs, axis=0))
    x2_ref[0] = x2
    hp_ref[0] = _rmsnorm_cols(x2, gffn_ref[...]).astype(hp_ref.dtype)


def _merge(x, attnT, gattnT, sgpT, waoT, woutT, gxa, wqT, kmem, vmemT, woT, gffn, tm):
    b, s, d = x.shape
    m = kmem.shape[1]
    tok = pl.BlockSpec((1, d, tm), lambda i, j: (i, 0, j))
    wspec = _const_spec((d, d))
    col = _const_spec((d, 1))
    return pl.pallas_call(
        _merge_kernel,
        out_shape=(jax.ShapeDtypeStruct((b, d, s), jnp.float32), jax.ShapeDtypeStruct((b, d, s), MXU_DTYPE)),
        grid=(b, s // tm),
        in_specs=[pl.BlockSpec((1, tm, d), lambda i, j: (i, j, 0)), tok, tok, tok, wspec, wspec, col, wspec,
                  pl.BlockSpec((1, m, d), lambda i, j: (i, 0, 0)), pl.BlockSpec((1, d, m), lambda i, j: (i, 0, 0)),
                  wspec, col],
        out_specs=(tok, tok),
        compiler_params=_params(("parallel", "parallel"), "merge"),
        name="merge",
    )(x, attnT, gattnT, sgpT, waoT, woutT, gxa, wqT, kmem, vmemT, woT, gffn)


def _sort16_network():
    def merge(lo, hi, r):
        step = 2 * r
        if step < hi - lo:
            yield from merge(lo, hi, step)
            yield from merge(lo + r, hi, step)
            yield from ((i, i + r) for i in range(lo + r, hi - r, step))
        else:
            yield (lo, lo + r)

    def sort(lo, hi):
        if hi > lo:
            mid = lo + (hi - lo) // 2
            yield from sort(lo, mid)
            yield from sort(mid + 1, hi)
            yield from merge(lo, hi, 1)

    return tuple(sort(0, PEER_TOPK - 1))


def _merge16_network():
    out, k = [], PEER_TOPK // 2
    while k >= 1:
        out += [(i, i + k) for i in range(PEER_TOPK) if i % (2 * k) < k]
        k //= 2
    return tuple(out)


_SORT16 = _sort16_network()
_MERGE16 = _merge16_network()


def _exchange(x, net):
    for i, j in net:
        x[i], x[j] = jnp.maximum(x[i], x[j]), jnp.minimum(x[i], x[j])
    return x


def _top_values(s, top_ref):
    assert s.shape[0] == PEER_TOPK * SUBLANES
    x = _exchange([s[v * SUBLANES:(v + 1) * SUBLANES, :] for v in range(PEER_TOPK)], _SORT16)
    shift = SUBLANES // 2
    while shift >= 1:
        x = _exchange([jnp.maximum(x[v], pltpu.roll(x[PEER_TOPK - 1 - v], shift, 0)) for v in range(PEER_TOPK)],
                      _MERGE16)
        shift //= 2
    for r in range(PEER_TOPK):
        top_ref[r:r + 1, :] = x[r][0:1, :]


def _stair_candidates(v1, v2):
    half = PEER_TOPK // 2
    row8 = lax.broadcasted_iota(jnp.int32, (half, 1), 0)
    cands = [v1 + v2[0:1], v1[0:1] + v2[half:]]
    for bb in range(1, half):
        a_max = PEER_TOPK // (bb + 1) - 1
        c = v1[:half] + v2[bb:bb + 1]
        cands.append(jnp.where(row8 <= a_max, c, NEG_INF))
    return jnp.concatenate(cands, axis=0)


def _peer_sel_kernel(hp_ref, wq_ref, k1_ref, k2_ref, nb_ref, p1_ref, r2_ref, p2_ref, v1_s, v2_s, c_s):
    qp = _dot(wq_ref[...], hp_ref[0]).astype(MXU_DTYPE)
    for hd in range(PEER_HEADS):
        base = hd * PEER_D_KEY
        s1 = _dot(k1_ref[hd], qp[base:base + PEER_HALF])
        s2 = _dot(k2_ref[hd], qp[base + PEER_HALF:base + PEER_D_KEY])
        _top_values(s1, v1_s)
        _top_values(s2, v2_s)
        v1 = v1_s[...]
        v2 = v2_s[...]
        cand = _stair_candidates(v1, v2)
        pad = jnp.full((PEER_N_KEYS - cand.shape[0], cand.shape[1]), NEG_INF, jnp.float32)
        _top_values(jnp.concatenate([cand, pad], axis=0), c_s)
        t = c_s[PEER_TOPK - 1:PEER_TOPK, :]
        m1 = v1[0:1]
        m2 = v2[0:1]
        z = jnp.sum(jnp.where(cand >= t, jnp.exp(cand - (m1 + m2)), 0.0), axis=0, keepdims=True)
        nb_top = jnp.zeros_like(v1)
        for b in range(PEER_TOPK):
            nb_top = nb_top + jnp.where(v1 + v2[b:b + 1] >= t, 1.0, 0.0)
        nb = jnp.zeros_like(s1)
        r2 = jnp.full(s2.shape, float(PEER_TOPK), jnp.float32)
        for a in reversed(range(PEER_TOPK)):
            nb = jnp.where(s1 == v1[a:a + 1], nb_top[a:a + 1], nb)
            r2 = jnp.where(s2 == v2[a:a + 1], float(a), r2)
        nb_ref[0, hd] = _pair_words(nb)
        p1_ref[0, hd] = _pair_words(jnp.exp(s1 - m1) / z)
        r2_ref[0, hd] = r2.astype(r2_ref.dtype)
        p2_ref[0, hd] = jnp.exp(s2 - m2).astype(p2_ref.dtype)


def _pair_words(x):
    u = lax.bitcast_convert_type(x.astype(GATE_DTYPE).astype(jnp.float32), jnp.uint32)
    return u | (u >> 16)


def _peer_sel(hpT, wpqT, k1, k2, tm):
    b, d, s = hpT.shape
    shape = (b, PEER_HEADS, PEER_N_KEYS, s)
    words = jax.ShapeDtypeStruct(shape, jnp.uint32)
    sel = jax.ShapeDtypeStruct(shape, GATE_DTYPE)
    sel_spec = pl.BlockSpec((1, PEER_HEADS, PEER_N_KEYS, tm), lambda i, j: (i, 0, 0, j))
    return pl.pallas_call(
        _peer_sel_kernel,
        out_shape=(words, words, sel, sel),
        grid=(b, s // tm),
        in_specs=[pl.BlockSpec((1, d, tm), lambda i, j: (i, 0, j)), _const_spec(wpqT.shape),
                  _const_spec(k1.shape), _const_spec(k2.shape)],
        out_specs=(sel_spec, sel_spec, sel_spec, sel_spec),
        scratch_shapes=[pltpu.VMEM((PEER_TOPK, tm), jnp.float32) for _ in range(3)],
        compiler_params=_params(("parallel", "parallel"), "peer_sel"),
        name="peer_sel",
    )(hpT, wpqT, k1, k2)


def _peer_mix_kernel(hp_ref, ed_ref, euT_ref, nb_ref, p1_ref, r2_ref, p2_ref, x2_ref, gfin_ref, y_ref,
                     acc_ref, act_ref, w_ref, row_s, key_s):
    c = pl.program_id(2)
    n_c = pl.num_programs(2)
    tt = hp_ref.shape[2]
    n_i = ed_ref.shape[0] // PEER_N_KEYS
    jb = GATE_ROWS
    n_k = PEER_N_KEYS // jb

    @pl.when(c == 0)
    def _():
        acc_ref[...] = jnp.zeros_like(acc_ref)
        key_s[0] = r2_ref[0]
        key_s[1] = p2_ref[0]

    row_s[0] = nb_ref[0]
    row_s[1] = p1_ref[0]

    act_ref[...] = _gelu(_dot(ed_ref[...], hp_ref[0])).astype(act_ref.dtype)

    def row_bcast(words):
        return pltpu.bitcast(jnp.broadcast_to(words, (SUBLANES, LANES)), GATE_DTYPE)

    zero = jnp.zeros((jb, LANES), GATE_DTYPE)

    for lb in range(tt // LANES):
        ls = pl.ds(lb * LANES, LANES)
        for il in range(n_i):
            accs = [None] * n_k
            for hd in range(PEER_HEADS):
                bn = row_bcast(row_s[0, hd, il:il + 1, ls])
                bp = row_bcast(row_s[1, hd, il:il + 1, ls])
                for k in range(n_k):
                    r2v = key_s[0, hd, k * jb:(k + 1) * jb, ls]
                    p2v = key_s[1, hd, k * jb:(k + 1) * jb, ls]
                    term = bp * jnp.where(r2v < bn, p2v, zero)
                    accs[k] = term if accs[k] is None else accs[k] + term
            for k in range(n_k):
                rows = pl.ds(il * PEER_N_KEYS + k * jb, jb)
                w_ref[rows, ls] = (accs[k] * act_ref[rows, ls]).astype(w_ref.dtype)

    acc_ref[...] += _dot(euT_ref[0], w_ref[...])

    @pl.when(c == n_c - 1)
    def _():
        x3 = x2_ref[0] + acc_ref[...]
        y_ref[0] = _rmsnorm_cols(x3, gfin_ref[...]).T.astype(y_ref.dtype)


def _peer_mix(hpT, e_down, e_upT, nb, p1, r2, p2, x2T, gfin, tt):
    b, d, s = hpT.shape
    n_e = e_down.shape[0]
    ec = PEER_CHUNK
    n_i = ec // PEER_N_KEYS
    tok = lambda rows: pl.BlockSpec((1, rows, tt), lambda i, j, c: (i, 0, j))
    row_spec = pl.BlockSpec((1, PEER_HEADS, n_i, tt), lambda i, j, c: (i, 0, c, j))
    full_spec = pl.BlockSpec((1, PEER_HEADS, PEER_N_KEYS, tt), lambda i, j, c: (i, 0, 0, j))
    return pl.pallas_call(
        _peer_mix_kernel,
        out_shape=jax.ShapeDtypeStruct((b, s, d), jnp.float32),
        grid=(b, s // tt, n_e // ec),
        in_specs=[tok(d),
                  pl.BlockSpec((ec, d), lambda i, j, c: (c, 0)),
                  pl.BlockSpec((1, d, ec), lambda i, j, c: (c, 0, 0)),
                  row_spec, row_spec, full_spec, full_spec, tok(d), _const_spec((d, 1))],
        out_specs=pl.BlockSpec((1, tt, d), lambda i, j, c: (i, j, 0)),
        scratch_shapes=[pltpu.VMEM((d, tt), jnp.float32), pltpu.VMEM((ec, tt), GATE_DTYPE),
                        pltpu.VMEM((ec, tt), MXU_DTYPE),
                        pltpu.VMEM((2, PEER_HEADS, n_i, tt), jnp.uint32),
                        pltpu.VMEM((2, PEER_HEADS, PEER_N_KEYS, tt), GATE_DTYPE)],
        compiler_params=_params(("parallel", "parallel", "arbitrary"), "peer_mix"),
        name="peer_mix",
    )(hpT, e_down, e_upT, nb, p1, r2, p2, x2T, gfin)


def _rope_tables(s):
    rows = s // GRID_W
    row_pos = jnp.repeat(jnp.arange(rows, dtype=jnp.float32), GRID_W)
    col_pos = jnp.tile(jnp.arange(GRID_W, dtype=jnp.float32), rows)
    n_freq = HEAD_DIM // 4
    inv_freq = ROPE_THETA ** (-jnp.arange(n_freq, dtype=jnp.float32) / n_freq)
    ang_r = inv_freq[:, None] * row_pos[None, :]
    ang_c = inv_freq[:, None] * col_pos[None, :]
    cos_t = jnp.concatenate([jnp.cos(ang_r), jnp.cos(ang_r), jnp.cos(ang_c), jnp.cos(ang_c)], axis=0)
    sin_t = jnp.concatenate([-jnp.sin(ang_r), jnp.sin(ang_r), -jnp.sin(ang_c), jnp.sin(ang_c)], axis=0)
    return cos_t, sin_t


def _tiles(s):
    attn = max(SG_CHUNK, min(1024, ATTN_SCORE_BYTES // (2 * 2 * s)))
    return dict(inproj=min(512, s), attn=min(attn, s), merge=min(512, s), sel=min(256, s), mix=min(512, s))


def _layer(x, kmem, vmemT, w):
    s = x.shape[1]
    t = _tiles(s)
    cos_t, sin_t = _rope_tables(s)
    qT, k, vT, gattnT, sgpT = _inproj(x, w["gmix"], w["w_inT"], w["gq"], w["gk"], cos_t, sin_t, w["gsg"],
                                      w["wsT"], w["bs"], w["wsgoT"], t["inproj"])
    attnT = _attention(qT, k, vT, t["attn"])
    x2T, hpT = _merge(x, attnT, gattnT, sgpT, w["waoT"], w["woutT"], w["gxa"], w["wqxT"], kmem, vmemT,
                      w["woxT"], w["gffn"], t["merge"])
    nb, p1, r2, p2 = _peer_sel(hpT, w["wpqT"], w["k1"], w["k2"], t["sel"])
    return _peer_mix(hpT, w["e_down"], w["e_upT"], nb, p1, r2, p2, x2T, w["gfin"], t["mix"])


def kernel(x_prompt, x_sample, mem_prompt, mem_sample, norm_mix_g, w_in, q_norm_g, k_norm_g, sg_norm_g, sg_w,
           sg_b, w_attn_o, w_sg_o, w_out, norm_xa_g, norm_mem_g, wq_xa, wkv_xa, wo_xa, norm_ffn_g, w_peer_q,
           peer_k1, peer_k2, expert_down, expert_up, final_norm_g):
    assert w_in.shape[0] == 1, "single-layer trunk"
    w = _prep_weights(norm_mix_g[0], w_in[0], q_norm_g[0], k_norm_g[0], sg_norm_g[0], sg_w[0], sg_b[0],
                      w_attn_o[0], w_sg_o[0], w_out[0], norm_xa_g[0], norm_mem_g[0], wq_xa[0], wkv_xa[0],
                      wo_xa[0], norm_ffn_g[0], w_peer_q[0], peer_k1[0], peer_k2[0], expert_down[0],
                      expert_up[0], final_norm_g)
    outs = []
    for x, mem in ((x_prompt, mem_prompt), (x_sample, mem_sample)):
        kmem, vmemT = _kv_mem(mem, w["gmem"], w["wk"], w["wvT"])
        outs.append(_layer(x, kmem, vmemT, w))
    return tuple(outs)


def _prep_weights(norm_mix_g, w_in, q_norm_g, k_norm_g, sg_norm_g, sg_w, sg_b, w_attn_o, w_sg_o, w_out,
                  norm_xa_g, norm_mem_g, wq_xa, wkv_xa, wo_xa, norm_ffn_g, w_peer_q, peer_k1, peer_k2,
                  expert_down, expert_up, final_norm_g):
    cast_t = lambda a: a.astype(MXU_DTYPE).T
    col = lambda g: g.reshape(-1, 1).astype(jnp.float32)
    n_e, d = expert_up.shape
    e_upT = jnp.swapaxes(expert_up.astype(MXU_DTYPE).reshape(n_e // PEER_CHUNK, PEER_CHUNK, d), 1, 2)
    return dict(
        gmix=norm_mix_g.reshape(1, -1), w_inT=cast_t(w_in), gq=col(q_norm_g), gk=col(k_norm_g),
        gsg=col(sg_norm_g), wsT=jnp.swapaxes(sg_w, 1, 2).astype(MXU_DTYPE),
        bs=sg_b.reshape(SG_GROUPS, 1, SG_CHUNK), wsgoT=cast_t(w_sg_o),
        waoT=cast_t(w_attn_o), woutT=cast_t(w_out), gxa=col(norm_xa_g), wqxT=cast_t(wq_xa),
        woxT=cast_t(wo_xa), gffn=col(norm_ffn_g), wpqT=cast_t(w_peer_q),
        k1=peer_k1.astype(MXU_DTYPE), k2=peer_k2.astype(MXU_DTYPE),
        e_down=expert_down.astype(MXU_DTYPE), e_upT=e_upT, gfin=col(final_norm_g),
        gmem=norm_mem_g.reshape(1, -1), wk=wkv_xa[:, :D_MODEL].astype(MXU_DTYPE), wvT=cast_t(wkv_xa[:, D_MODEL:]),
    )
```

```python
import math

import jax
import jax.numpy as jnp
from jax import lax
from jax.experimental import pallas as pl
from jax.experimental.pallas import tpu as pltpu

D_MODEL = 1024
GRID_W = 64
N_HEADS = 16
N_KV_HEADS = 4
HEAD_DIM = 64
KV_GROUP = N_HEADS // N_KV_HEADS
ATTN_Q_WIDTH = N_HEADS * HEAD_DIM
ATTN_KV_WIDTH = N_KV_HEADS * HEAD_DIM
ROPE_THETA = 10000.0
SG_WIDTH = 1024
SG_GROUPS = 8
SG_GROUP_DIM = SG_WIDTH // SG_GROUPS
SG_CHUNK = 128
XA_HEADS = 4
XA_HEAD_DIM = D_MODEL // XA_HEADS
PEER_HEADS = 8
PEER_N_KEYS = 128
PEER_D_KEY = 256
PEER_HALF = PEER_D_KEY // 2
PEER_TOPK = 16
PEER_CHUNK = 1024
MXU_TILE = 256
MXU_BLOCK_ROWS = 512
MXU_SLOT = MXU_BLOCK_ROWS // 4
EPS = 1e-6

_Q0 = 0
_K0 = _Q0 + ATTN_Q_WIDTH
_V0 = _K0 + ATTN_KV_WIDTH
_Z0 = _V0 + ATTN_KV_WIDTH
_G0 = _Z0 + 2 * SG_WIDTH
_IN_WIDTH = _G0 + 2 * D_MODEL

LANES = 128
SUBLANES = 8
V7X_VMEM_BYTES = 64 * 1024 * 1024
MIB = 1024 * 1024
ATTN_SCORE_BYTES = 16 * MIB
VMEM_LIMIT_MIB = dict(kv_mem=32, inproj=56, attn=48, merge=48, peer_sel=48, peer_mix=56)
assert max(VMEM_LIMIT_MIB.values()) * MIB < V7X_VMEM_BYTES

MXU_DTYPE = jnp.bfloat16
GATE_DTYPE = jnp.bfloat16
GATE_ROWS = 2 * SUBLANES
NEG_INF = float("-inf")


def _dot(a, b):
    return jnp.dot(a, b, preferred_element_type=jnp.float32)


def _dot_nt(a, b):
    return lax.dot_general(a, b, (((1,), (1,)), ((), ())), preferred_element_type=jnp.float32)


def _gelu(x):
    c = math.sqrt(2.0 / math.pi)
    return 0.5 * x * (1.0 + jnp.tanh(c * (x + 0.044715 * (x * x * x))))


def _sigmoid(x):
    return 1.0 / (1.0 + jnp.exp(-x))


def _rmsnorm_rows(x, g_row):
    ms = jnp.mean(x * x, axis=-1, keepdims=True)
    return x * lax.rsqrt(ms + EPS) * g_row


def _rmsnorm_cols(xT, g_col):
    ms = jnp.mean(xT * xT, axis=0, keepdims=True)
    return xT * lax.rsqrt(ms + EPS) * g_col


def _params(semantics, stage):
    return pltpu.CompilerParams(dimension_semantics=semantics, vmem_limit_bytes=VMEM_LIMIT_MIB[stage] * MIB)


def _const_spec(shape):
    nd = len(shape)
    return pl.BlockSpec(shape, lambda *_: (0,) * nd)


def _kv_mem_kernel(mem_ref, g_ref, wk_ref, wvT_ref, k_ref, vT_ref):
    mn = _rmsnorm_rows(mem_ref[0], g_ref[...]).astype(MXU_DTYPE)
    k_ref[0] = _dot(mn, wk_ref[...]).astype(k_ref.dtype)
    vT_ref[0] = _dot_nt(wvT_ref[...], mn).astype(vT_ref.dtype)


def _kv_mem(mem, g_row, wk, wvT):
    nb, m, d = mem.shape
    return pl.pallas_call(
        _kv_mem_kernel,
        out_shape=(jax.ShapeDtypeStruct((nb, m, d), MXU_DTYPE), jax.ShapeDtypeStruct((nb, d, m), MXU_DTYPE)),
        grid=(nb,),
        in_specs=[pl.BlockSpec((1, m, d), lambda b: (b, 0, 0)), _const_spec((1, d)),
                  _const_spec((d, d)), _const_spec((d, d))],
        out_specs=(pl.BlockSpec((1, m, d), lambda b: (b, 0, 0)), pl.BlockSpec((1, d, m), lambda b: (b, 0, 0))),
        compiler_params=_params(("parallel",), "kv_mem"),
        name="kv_mem",
    )(mem, g_row, wk, wvT)


def _head_norm_rope(t, g_col, cos, sin):
    ms = jnp.mean(t * t, axis=0, keepdims=True)
    t = t * lax.rsqrt(ms + EPS) * g_col
    q4 = HEAD_DIM // 4
    sw = jnp.concatenate([t[q4:2 * q4], t[0:q4], t[3 * q4:], t[2 * q4:3 * q4]], axis=0)
    return t * cos + sw * sin


def _inproj_kernel(x_ref, gmix_ref, w_ref, gq_ref, gk_ref, cos_ref, sin_ref, gsg_ref, wsT_ref, bs_ref, wsgo_ref,
                   qT_ref, k_ref, vT_ref, gattn_ref, sgp_ref):
    tm = x_ref.shape[1]
    h = _rmsnorm_rows(x_ref[0], gmix_ref[...]).astype(MXU_DTYPE)

    def proj_t(lo, hi):
        return _dot_nt(w_ref[lo:hi, :], h)

    cos = cos_ref[...]
    sin = sin_ref[...]

    q_t = proj_t(_Q0, _K0)
    scale = math.log2(math.e) / math.sqrt(HEAD_DIM)
    for hd in range(N_HEADS):
        r = _head_norm_rope(q_t[hd * HEAD_DIM:(hd + 1) * HEAD_DIM], gq_ref[...], cos, sin) * scale
        qT_ref[0, hd * HEAD_DIM:(hd + 1) * HEAD_DIM, :] = r.astype(qT_ref.dtype)

    k_t = proj_t(_K0, _V0)
    k_rot = jnp.concatenate(
        [_head_norm_rope(k_t[g * HEAD_DIM:(g + 1) * HEAD_DIM], gk_ref[...], cos, sin) for g in range(N_KV_HEADS)],
        axis=0)
    k_tok = k_rot.T
    for g in range(N_KV_HEADS):
        k_ref[0, g] = k_tok[:, g * HEAD_DIM:(g + 1) * HEAD_DIM].astype(k_ref.dtype)

    vT_ref[0] = proj_t(_V0, _Z0).astype(vT_ref.dtype)

    z_t = _gelu(proj_t(_Z0, _G0))
    u = z_t[:SG_WIDTH]
    vn = _rmsnorm_cols(z_t[SG_WIDTH:], gsg_ref[...]).astype(MXU_DTYPE)
    n_chunks = tm // SG_CHUNK
    sv_groups = []
    for g in range(SG_GROUPS):
        vg = vn[g * SG_GROUP_DIM:(g + 1) * SG_GROUP_DIM]
        lhs = jnp.concatenate([vg[:, c * SG_CHUNK:(c + 1) * SG_CHUNK] for c in range(n_chunks)], axis=0)
        r = _dot(lhs, wsT_ref[g]) + bs_ref[g]
        sv_groups.append(jnp.concatenate(
            [r[c * SG_GROUP_DIM:(c + 1) * SG_GROUP_DIM] for c in range(n_chunks)], axis=1))
    sg = (u * jnp.concatenate(sv_groups, axis=0)).astype(MXU_DTYPE)
    sg_branch = _dot(wsgo_ref[...], sg)

    gates = _sigmoid(proj_t(_G0, _IN_WIDTH))
    gattn_ref[0] = gates[:D_MODEL].astype(gattn_ref.dtype)
    sgp_ref[0] = (gates[D_MODEL:] * sg_branch).astype(sgp_ref.dtype)


def _inproj(x, gmix, w_inT, gq, gk, cos_t, sin_t, gsg, wsT, bs, wsgoT, tm):
    b, s, d = x.shape
    grid = (b, s // tm)
    tok = lambda rows: pl.BlockSpec((1, rows, tm), lambda i, j: (i, 0, j))
    return pl.pallas_call(
        _inproj_kernel,
        out_shape=(jax.ShapeDtypeStruct((b, ATTN_Q_WIDTH, s), MXU_DTYPE),
                   jax.ShapeDtypeStruct((b, N_KV_HEADS, s, HEAD_DIM), MXU_DTYPE),
                   jax.ShapeDtypeStruct((b, ATTN_KV_WIDTH, s), MXU_DTYPE),
                   jax.ShapeDtypeStruct((b, D_MODEL, s), MXU_DTYPE),
                   jax.ShapeDtypeStruct((b, D_MODEL, s), jnp.float32)),
        grid=grid,
        in_specs=[pl.BlockSpec((1, tm, d), lambda i, j: (i, j, 0)),
                  _const_spec((1, d)), _const_spec(w_inT.shape),
                  _const_spec((HEAD_DIM, 1)), _const_spec((HEAD_DIM, 1)),
                  pl.BlockSpec((HEAD_DIM, tm), lambda i, j: (0, j)),
                  pl.BlockSpec((HEAD_DIM, tm), lambda i, j: (0, j)),
                  _const_spec((SG_WIDTH, 1)), _const_spec(wsT.shape), _const_spec(bs.shape),
                  _const_spec(wsgoT.shape)],
        out_specs=(tok(ATTN_Q_WIDTH),
                   pl.BlockSpec((1, N_KV_HEADS, tm, HEAD_DIM), lambda i, j: (i, 0, j, 0)),
                   tok(ATTN_KV_WIDTH), tok(D_MODEL), tok(D_MODEL)),
        compiler_params=_params(("parallel", "parallel"), "inproj"),
        name="inproj",
    )(x, gmix, w_inT, gq, gk, cos_t, sin_t, gsg, wsT, bs, wsgoT)


def _attn_kernel(qT_ref, k_ref, vT_ref, o_ref):
    tq = qT_ref.shape[2]
    k = k_ref[0, 0]
    vT = vT_ref[0]
    vT1 = jnp.concatenate([vT, jnp.ones((GATE_ROWS, vT.shape[1]), vT.dtype)], axis=0)
    for pair in range(KV_GROUP // 2):
        heads = (2 * pair, 2 * pair + 1)
        qT2 = jnp.concatenate([qT_ref[0, a * HEAD_DIM:(a + 1) * HEAD_DIM, :] for a in heads], axis=1)
        sT = _dot(k, qT2).astype(MXU_DTYPE)
        m = jnp.max(sT, axis=0, keepdims=True)
        p = jnp.exp2(sT - m)
        o = _dot(vT1, p)
        oT = o[:HEAD_DIM] / o[HEAD_DIM:HEAD_DIM + 1]
        for i, a in enumerate(heads):
            o_ref[0, a * HEAD_DIM:(a + 1) * HEAD_DIM, :] = oT[:, i * tq:(i + 1) * tq].astype(o_ref.dtype)


def _attention(qT, k, vT, tq):
    b, _, s = qT.shape
    rows = KV_GROUP * HEAD_DIM
    return pl.pallas_call(
        _attn_kernel,
        out_shape=jax.ShapeDtypeStruct((b, ATTN_Q_WIDTH, s), MXU_DTYPE),
        grid=(b, N_KV_HEADS, s // tq),
        in_specs=[pl.BlockSpec((1, rows, tq), lambda i, g, j: (i, g, j)),
                  pl.BlockSpec((1, 1, s, HEAD_DIM), lambda i, g, j: (i, g, 0, 0)),
                  pl.BlockSpec((1, HEAD_DIM, s), lambda i, g, j: (i, g, 0))],
        out_specs=pl.BlockSpec((1, rows, tq), lambda i, g, j: (i, g, j)),
        compiler_params=_params(("parallel", "parallel", "parallel"), "attn"),
        name="attn",
    )(qT, k, vT)


def _merge_kernel(x_ref, attn_ref, gattn_ref, sgp_ref, wao_ref, wout_ref, gxa_ref, wq_ref, km_ref, vmT_ref,
                  wo_ref, gffn_ref, x2_ref, hp_ref):
    xT = x_ref[0].T
    attn_branch = _dot(wao_ref[...], attn_ref[0])
    mix = gattn_ref[0].astype(jnp.float32) * attn_branch + sgp_ref[0]
    x1 = xT + _dot(wout_ref[...], mix.astype(MXU_DTYPE))

    hx = _rmsnorm_cols(x1, gxa_ref[...]).astype(MXU_DTYPE)
    qx = (_dot(wq_ref[...], hx) * (1.0 / math.sqrt(XA_HEAD_DIM))).astype(MXU_DTYPE)
    outs = []
    for hd in range(XA_HEADS):
        lo, hi = hd * XA_HEAD_DIM, (hd + 1) * XA_HEAD_DIM
        sT = _dot(km_ref[0, :, lo:hi], qx[lo:hi])
        m = jnp.max(sT, axis=0, keepdims=True)
        p = jnp.exp(sT - m)
        l = jnp.sum(p, axis=0, keepdims=True)
        outs.append((_dot(vmT_ref[0, lo:hi, :], p.astype(MXU_DTYPE)) / l).astype(MXU_DTYPE))
    x2 = x1 + _dot(wo_ref[...], jnp.concatenate(outs, axis=0))
    x2_ref[0] = x2
    hp_ref[0] = _rmsnorm_cols(x2, gffn_ref[...]).astype(hp_ref.dtype)


def _merge(x, attnT, gattnT, sgpT, waoT, woutT, gxa, wqT, kmem, vmemT, woT, gffn, tm):
    b, s, d = x.shape
    m = kmem.shape[1]
    tok = pl.BlockSpec((1, d, tm), lambda i, j: (i, 0, j))
    wspec = _const_spec((d, d))
    col = _const_spec((d, 1))
    return pl.pallas_call(
        _merge_kernel,
        out_shape=(jax.ShapeDtypeStruct((b, d, s), jnp.float32), jax.ShapeDtypeStruct((b, d, s), MXU_DTYPE)),
        grid=(b, s // tm),
        in_specs=[pl.BlockSpec((1, tm, d), lambda i, j: (i, j, 0)), tok, tok, tok, wspec, wspec, col, wspec,
                  pl.BlockSpec((1, m, d), lambda i, j: (i, 0, 0)), pl.BlockSpec((1, d, m), lambda i, j: (i, 0, 0)),
                  wspec, col],
        out_specs=(tok, tok),
        compiler_params=_params(("parallel", "parallel"), "merge"),
        name="merge",
    )(x, attnT, gattnT, sgpT, waoT, woutT, gxa, wqT, kmem, vmemT, woT, gffn)


def _sort16_network():
    def merge(lo, hi, r):
        step = 2 * r
        if step < hi - lo:
            yield from merge(lo, hi, step)
            yield from merge(lo + r, hi, step)
            yield from ((i, i + r) for i in range(lo + r, hi - r, step))
        else:
            yield (lo, lo + r)

    def sort(lo, hi):
        if hi > lo:
            mid = lo + (hi - lo) // 2
            yield from sort(lo, mid)
            yield from sort(mid + 1, hi)
            yield from merge(lo, hi, 1)

    return tuple(sort(0, PEER_TOPK - 1))


def _merge16_network():
    out, k = [], PEER_TOPK // 2
    while k >= 1:
        out += [(i, i + k) for i in range(PEER_TOPK) if i % (2 * k) < k]
        k //= 2
    return tuple(out)


_SORT16 = _sort16_network()
_MERGE16 = _merge16_network()


def _exchange(x, net):
    for i, j in net:
        x[i], x[j] = jnp.maximum(x[i], x[j]), jnp.minimum(x[i], x[j])
    return x


def _top_values(s, top_ref):
    assert s.shape[0] == PEER_TOPK * SUBLANES
    x = _exchange([s[v * SUBLANES:(v + 1) * SUBLANES, :] for v in range(PEER_TOPK)], _SORT16)
    shift = SUBLANES // 2
    while shift >= 1:
        x = _exchange([jnp.maximum(x[v], pltpu.roll(x[PEER_TOPK - 1 - v], shift, 0)) for v in range(PEER_TOPK)],
                      _MERGE16)
        shift //= 2
    for r in range(PEER_TOPK):
        top_ref[r:r + 1, :] = x[r][0:1, :]


def _stair_candidates(v1, v2):
    half = PEER_TOPK // 2
    row8 = lax.broadcasted_iota(jnp.int32, (half, 1), 0)
    cands = [v1 + v2[0:1], v1[0:1] + v2[half:]]
    for bb in range(1, half):
        a_max = PEER_TOPK // (bb + 1) - 1
        c = v1[:half] + v2[bb:bb + 1]
        cands.append(jnp.where(row8 <= a_max, c, NEG_INF))
    return jnp.concatenate(cands, axis=0)


def _peer_sel_kernel(hp_ref, wq_ref, k1_ref, k2_ref, nb_ref, p1_ref, r2_ref, p2_ref, v1_s, v2_s, c_s):
    qp = _dot(wq_ref[...], hp_ref[0]).astype(MXU_DTYPE)
    for hd in range(PEER_HEADS):
        base = hd * PEER_D_KEY
        s1 = _dot(k1_ref[hd], qp[base:base + PEER_HALF])
        s2 = _dot(k2_ref[hd], qp[base + PEER_HALF:base + PEER_D_KEY])
        _top_values(s1, v1_s)
        _top_values(s2, v2_s)
        v1 = v1_s[...]
        v2 = v2_s[...]
        cand = _stair_candidates(v1, v2)
        pad = jnp.full((PEER_N_KEYS - cand.shape[0], cand.shape[1]), NEG_INF, jnp.float32)
        _top_values(jnp.concatenate([cand, pad], axis=0), c_s)
        t = c_s[PEER_TOPK - 1:PEER_TOPK, :]
        m1 = v1[0:1]
        m2 = v2[0:1]
        z = jnp.sum(jnp.where(cand >= t, jnp.exp(cand - (m1 + m2)), 0.0), axis=0, keepdims=True)
        nb_top = jnp.zeros_like(v1)
        for b in range(PEER_TOPK):
            nb_top = nb_top + jnp.where(v1 + v2[b:b + 1] >= t, 1.0, 0.0)
        nb = jnp.zeros_like(s1)
        r2 = jnp.full(s2.shape, float(PEER_TOPK), jnp.float32)
        for a in reversed(range(PEER_TOPK)):
            nb = jnp.where(s1 == v1[a:a + 1], nb_top[a:a + 1], nb)
            r2 = jnp.where(s2 == v2[a:a + 1], float(a), r2)
        nb_ref[0, hd] = _pair_words(nb)
        p1_ref[0, hd] = _pair_words(jnp.exp(s1 - m1) / z)
        r2_ref[0, hd] = r2.astype(r2_ref.dtype)
        p2_ref[0, hd] = jnp.exp(s2 - m2).astype(p2_ref.dtype)


def _pair_words(x):
    u = lax.bitcast_convert_type(x.astype(GATE_DTYPE).astype(jnp.float32), jnp.uint32)
    return u | (u >> 16)


def _peer_sel(hpT, wpqT, k1, k2, tm):
    b, d, s = hpT.shape
    shape = (b, PEER_HEADS, PEER_N_KEYS, s)
    words = jax.ShapeDtypeStruct(shape, jnp.uint32)
    sel = jax.ShapeDtypeStruct(shape, GATE_DTYPE)
    sel_spec = pl.BlockSpec((1, PEER_HEADS, PEER_N_KEYS, tm), lambda i, j: (i, 0, 0, j))
    return pl.pallas_call(
        _peer_sel_kernel,
        out_shape=(words, words, sel, sel),
        grid=(b, s // tm),
        in_specs=[pl.BlockSpec((1, d, tm), lambda i, j: (i, 0, j)), _const_spec(wpqT.shape),
                  _const_spec(k1.shape), _const_spec(k2.shape)],
        out_specs=(sel_spec, sel_spec, sel_spec, sel_spec),
        scratch_shapes=[pltpu.VMEM((PEER_TOPK, tm), jnp.float32) for _ in range(3)],
        compiler_params=_params(("parallel", "parallel"), "peer_sel"),
        name="peer_sel",
    )(hpT, wpqT, k1, k2)


def _peer_mix_kernel(hp_ref, ed_ref, eu_lag_ref, eu_last_ref, nb_ref, p1_ref, r2_ref, p2_ref, x2_ref, gfin_ref,
                     y_ref, acc_ref, g_ref, w_ref, row_s, key_s):
    c = pl.program_id(2)
    n_c = pl.num_programs(2)
    tt = hp_ref.shape[2]
    n_i = ed_ref.shape[0] // PEER_N_KEYS
    jb = GATE_ROWS
    n_k = PEER_N_KEYS // jb
    assert tt == 2 * MXU_TILE and ed_ref.shape[0] == 2 * MXU_BLOCK_ROWS and acc_ref.shape[0] == 2 * MXU_BLOCK_ROWS

    @pl.when(c == 0)
    def _():
        acc_ref[...] = jnp.zeros_like(acc_ref)
        w_ref[...] = jnp.zeros_like(w_ref)
        key_s[0] = r2_ref[0]
        key_s[1] = p2_ref[0]

    row_s[0] = nb_ref[0]
    row_s[1] = p1_ref[0]

    def row_bcast(words):
        return pltpu.bitcast(jnp.broadcast_to(words, (SUBLANES, LANES)), GATE_DTYPE)

    zero = jnp.zeros((jb, LANES), GATE_DTYPE)

    def gates(il_lo, il_hi):
        for il in range(il_lo, il_hi):
            for lb in range(tt // LANES):
                ls = pl.ds(lb * LANES, LANES)
                accs = [None] * n_k
                for hd in range(PEER_HEADS):
                    bn = row_bcast(row_s[0, hd, il:il + 1, ls])
                    bp = row_bcast(row_s[1, hd, il:il + 1, ls])
                    for k in range(n_k):
                        r2v = key_s[0, hd, k * jb:(k + 1) * jb, ls]
                        p2v = key_s[1, hd, k * jb:(k + 1) * jb, ls]
                        term = bp * jnp.where(r2v < bn, p2v, zero)
                        accs[k] = term if accs[k] is None else accs[k] + term
                for k in range(n_k):
                    g_ref[pl.ds(il * PEER_N_KEYS + k * jb, jb), ls] = accs[k]

    def issue(lhs_ref, rhs_ref, blk, slot):
        n_kt = rhs_ref.shape[0] // MXU_TILE
        rows = pl.ds(blk * MXU_BLOCK_ROWS, MXU_BLOCK_ROWS)
        for mxu in range(2):
            for kt in range(n_kt):
                ks = pl.ds(kt * MXU_TILE, MXU_TILE)
                pltpu.matmul_push_rhs(rhs_ref[ks, pl.ds(mxu * MXU_TILE, MXU_TILE)], kt % 2, mxu)
                pltpu.matmul_acc_lhs(slot * MXU_SLOT, lhs_ref[rows, ks], mxu, load_staged_rhs=kt % 2)

    def pop(slot, mxu):
        return pltpu.matmul_pop(slot * MXU_SLOT, (MXU_BLOCK_ROWS, MXU_TILE), jnp.float32, mxu)

    hp = hp_ref.at[0]
    eu = eu_lag_ref.at[0]
    quarter = n_i // 4
    for blk in range(2):
        issue(eu, w_ref, blk, blk)
        gates(blk * quarter, (blk + 1) * quarter)
        rows = pl.ds(blk * MXU_BLOCK_ROWS, MXU_BLOCK_ROWS)
        for mxu in range(2):
            cols = pl.ds(mxu * MXU_TILE, MXU_TILE)
            acc_ref[rows, cols] += pop(blk, mxu)
    acts = []
    for blk in range(2):
        issue(ed_ref, hp, blk, blk)
        gates((2 + blk) * quarter, (3 + blk) * quarter)
        acts.append([_gelu(pop(blk, mxu)).astype(GATE_DTYPE) for mxu in range(2)])
    for blk in range(2):
        rows = pl.ds(blk * MXU_BLOCK_ROWS, MXU_BLOCK_ROWS)
        for mxu in range(2):
            cols = pl.ds(mxu * MXU_TILE, MXU_TILE)
            w_ref[rows, cols] = (g_ref[rows, cols] * acts[blk][mxu]).astype(w_ref.dtype)

    @pl.when(c == n_c - 1)
    def _():
        eul = eu_last_ref.at[0]
        for blk in range(2):
            issue(eul, w_ref, blk, blk)
        for blk in range(2):
            rows = pl.ds(blk * MXU_BLOCK_ROWS, MXU_BLOCK_ROWS)
            for mxu in range(2):
                cols = pl.ds(mxu * MXU_TILE, MXU_TILE)
                acc_ref[rows, cols] += pop(blk, mxu)
        x3 = x2_ref[0] + acc_ref[...]
        y_ref[0] = _rmsnorm_cols(x3, gfin_ref[...]).T.astype(y_ref.dtype)


def _peer_mix(hpT, e_down, e_upT, nb, p1, r2, p2, x2T, gfin, tt):
    b, d, s = hpT.shape
    n_e = e_down.shape[0]
    ec = PEER_CHUNK
    n_i = ec // PEER_N_KEYS
    tok = lambda rows: pl.BlockSpec((1, rows, tt), lambda i, j, c: (i, 0, j))
    row_spec = pl.BlockSpec((1, PEER_HEADS, n_i, tt), lambda i, j, c: (i, 0, c, j))
    full_spec = pl.BlockSpec((1, PEER_HEADS, PEER_N_KEYS, tt), lambda i, j, c: (i, 0, 0, j))
    return pl.pallas_call(
        _peer_mix_kernel,
        out_shape=jax.ShapeDtypeStruct((b, s, d), jnp.float32),
        grid=(b, s // tt, n_e // ec),
        in_specs=[tok(d),
                  pl.BlockSpec((ec, d), lambda i, j, c: (c, 0)),
                  pl.BlockSpec((1, d, ec), lambda i, j, c: (jnp.maximum(c - 1, 0), 0, 0)),
                  pl.BlockSpec((1, d, ec), lambda i, j, c: (n_e // ec - 1, 0, 0)),
                  row_spec, row_spec, full_spec, full_spec, tok(d), _const_spec((d, 1))],
        out_specs=pl.BlockSpec((1, tt, d), lambda i, j, c: (i, j, 0)),
        scratch_shapes=[pltpu.VMEM((d, tt), jnp.float32), pltpu.VMEM((ec, tt), GATE_DTYPE),
                        pltpu.VMEM((ec, tt), MXU_DTYPE),
                        pltpu.VMEM((2, PEER_HEADS, n_i, tt), jnp.uint32),
                        pltpu.VMEM((2, PEER_HEADS, PEER_N_KEYS, tt), GATE_DTYPE)],
        compiler_params=_params(("parallel", "parallel", "arbitrary"), "peer_mix"),
        name="peer_mix",
    )(hpT, e_down, e_upT, e_upT, nb, p1, r2, p2, x2T, gfin)


def _rope_tables(s):
    rows = s // GRID_W
    row_pos = jnp.repeat(jnp.arange(rows, dtype=jnp.float32), GRID_W)
    col_pos = jnp.tile(jnp.arange(GRID_W, dtype=jnp.float32), rows)
    n_freq = HEAD_DIM // 4
    inv_freq = ROPE_THETA ** (-jnp.arange(n_freq, dtype=jnp.float32) / n_freq)
    ang_r = inv_freq[:, None] * row_pos[None, :]
    ang_c = inv_freq[:, None] * col_pos[None, :]
    cos_t = jnp.concatenate([jnp.cos(ang_r), jnp.cos(ang_r), jnp.cos(ang_c), jnp.cos(ang_c)], axis=0)
    sin_t = jnp.concatenate([-jnp.sin(ang_r), jnp.sin(ang_r), -jnp.sin(ang_c), jnp.sin(ang_c)], axis=0)
    return cos_t, sin_t


def _tiles(s):
    attn = max(SG_CHUNK, ATTN_SCORE_BYTES // (2 * 2 * s))
    return dict(inproj=min(512, s), attn=min(attn, s), merge=min(512, s), sel=min(256, s), mix=min(512, s))


def _layer(x, kmem, vmemT, w):
    s = x.shape[1]
    t = _tiles(s)
    cos_t, sin_t = _rope_tables(s)
    qT, k, vT, gattnT, sgpT = _inproj(x, w["gmix"], w["w_inT"], w["gq"], w["gk"], cos_t, sin_t, w["gsg"],
                                      w["wsT"], w["bs"], w["wsgoT"], t["inproj"])
    attnT = _attention(qT, k, vT, t["attn"])
    x2T, hpT = _merge(x, attnT, gattnT, sgpT, w["waoT"], w["woutT"], w["gxa"], w["wqxT"], kmem, vmemT,
                      w["woxT"], w["gffn"], t["merge"])
    nb, p1, r2, p2 = _peer_sel(hpT, w["wpqT"], w["k1"], w["k2"], t["sel"])
    return _peer_mix(hpT, w["e_down"], w["e_upT"], nb, p1, r2, p2, x2T, w["gfin"], t["mix"])


def kernel(x_prompt, x_sample, mem_prompt, mem_sample, norm_mix_g, w_in, q_norm_g, k_norm_g, sg_norm_g, sg_w,
           sg_b, w_attn_o, w_sg_o, w_out, norm_xa_g, norm_mem_g, wq_xa, wkv_xa, wo_xa, norm_ffn_g, w_peer_q,
           peer_k1, peer_k2, expert_down, expert_up, final_norm_g):
    assert w_in.shape[0] == 1, "single-layer trunk"
    w = _prep_weights(norm_mix_g[0], w_in[0], q_norm_g[0], k_norm_g[0], sg_norm_g[0], sg_w[0], sg_b[0],
                      w_attn_o[0], w_sg_o[0], w_out[0], norm_xa_g[0], norm_mem_g[0], wq_xa[0], wkv_xa[0],
                      wo_xa[0], norm_ffn_g[0], w_peer_q[0], peer_k1[0], peer_k2[0], expert_down[0],
                      expert_up[0], final_norm_g)
    outs = []
    for x, mem in ((x_prompt, mem_prompt), (x_sample, mem_sample)):
        kmem, vmemT = _kv_mem(mem, w["gmem"], w["wk"], w["wvT"])
        outs.append(_layer(x, kmem, vmemT, w))
    return tuple(outs)


def _prep_weights(norm_mix_g, w_in, q_norm_g, k_norm_g, sg_norm_g, sg_w, sg_b, w_attn_o, w_sg_o, w_out,
                  norm_xa_g, norm_mem_g, wq_xa, wkv_xa, wo_xa, norm_ffn_g, w_peer_q, peer_k1, peer_k2,
                  expert_down, expert_up, final_norm_g):
    cast_t = lambda a: a.astype(MXU_DTYPE).T
    col = lambda g: g.reshape(-1, 1).astype(jnp.float32)
    n_e, d = expert_up.shape
    e_upT = jnp.swapaxes(expert_up.astype(MXU_DTYPE).reshape(n_e // PEER_CHUNK, PEER_CHUNK, d), 1, 2)
    return dict(
        gmix=norm_mix_g.reshape(1, -1), w_inT=cast_t(w_in), gq=col(q_norm_g), gk=col(k_norm_g),
        gsg=col(sg_norm_g), wsT=jnp.swapaxes(sg_w, 1, 2).astype(MXU_DTYPE),
        bs=sg_b.reshape(SG_GROUPS, 1, SG_CHUNK), wsgoT=cast_t(w_sg_o),
        waoT=cast_t(w_attn_o), woutT=cast_t(w_out), gxa=col(norm_xa_g), wqxT=cast_t(wq_xa),
        woxT=cast_t(wo_xa), gffn=col(norm_ffn_g), wpqT=cast_t(w_peer_q),
        k1=peer_k1.astype(MXU_DTYPE), k2=peer_k2.astype(MXU_DTYPE),
        e_down=expert_down.astype(MXU_DTYPE), e_upT=e_upT, gfin=col(final_norm_g),
        gmem=norm_mem_g.reshape(1, -1), wk=wkv_xa[:, :D_MODEL].astype(MXU_DTYPE), wvT=cast_t(wkv_xa[:, D_MODEL:]),
    )
```

```python
import math

import jax
import jax.numpy as jnp
from jax import lax
from jax.experimental import pallas as pl
from jax.experimental.pallas import tpu as pltpu

D_MODEL = 1024
GRID_W = 64
N_HEADS = 16
N_KV_HEADS = 4
HEAD_DIM = 64
KV_GROUP = N_HEADS // N_KV_HEADS
ATTN_Q_WIDTH = N_HEADS * HEAD_DIM
ATTN_KV_WIDTH = N_KV_HEADS * HEAD_DIM
ROPE_THETA = 10000.0
SG_WIDTH = 1024
SG_GROUPS = 8
SG_GROUP_DIM = SG_WIDTH // SG_GROUPS
SG_CHUNK = 128
XA_HEADS = 4
XA_HEAD_DIM = D_MODEL // XA_HEADS
PEER_HEADS = 8
PEER_N_KEYS = 128
PEER_D_KEY = 256
PEER_HALF = PEER_D_KEY // 2
PEER_TOPK = 16
PEER_CHUNK = 2048
EPS = 1e-6

_Q0 = 0
_K0 = _Q0 + ATTN_Q_WIDTH
_V0 = _K0 + ATTN_KV_WIDTH
_Z0 = _V0 + ATTN_KV_WIDTH
_G0 = _Z0 + 2 * SG_WIDTH
_IN_WIDTH = _G0 + 2 * D_MODEL

LANES = 128
SUBLANES = 8
V7X_VMEM_BYTES = 64 * 1024 * 1024
MIB = 1024 * 1024
ATTN_SCORE_BYTES = 16 * MIB
VMEM_LIMIT_MIB = dict(kv_mem=32, inproj=56, attn=48, merge=48, peer_sel=48, peer_mix=56)
assert max(VMEM_LIMIT_MIB.values()) * MIB < V7X_VMEM_BYTES

MXU_DTYPE = jnp.bfloat16
GATE_DTYPE = jnp.bfloat16
GATE_ROWS = 2 * SUBLANES
NEG_INF = float("-inf")


def _dot(a, b):
    return jnp.dot(a, b, preferred_element_type=jnp.float32)


def _dot_nt(a, b):
    return lax.dot_general(a, b, (((1,), (1,)), ((), ())), preferred_element_type=jnp.float32)


def _gelu(x):
    c = math.sqrt(2.0 / math.pi)
    return 0.5 * x * (1.0 + jnp.tanh(c * (x + 0.044715 * (x * x * x))))


def _sigmoid(x):
    return 1.0 / (1.0 + jnp.exp(-x))


def _rmsnorm_rows(x, g_row):
    ms = jnp.mean(x * x, axis=-1, keepdims=True)
    return x * lax.rsqrt(ms + EPS) * g_row


def _rmsnorm_cols(xT, g_col):
    ms = jnp.mean(xT * xT, axis=0, keepdims=True)
    return xT * lax.rsqrt(ms + EPS) * g_col


def _params(semantics, stage):
    return pltpu.CompilerParams(dimension_semantics=semantics, vmem_limit_bytes=VMEM_LIMIT_MIB[stage] * MIB)


def _const_spec(shape):
    nd = len(shape)
    return pl.BlockSpec(shape, lambda *_: (0,) * nd)


def _kv_mem_kernel(mem_ref, g_ref, wk_ref, wvT_ref, k_ref, vT_ref):
    mn = _rmsnorm_rows(mem_ref[0], g_ref[...]).astype(MXU_DTYPE)
    k_ref[0] = _dot(mn, wk_ref[...]).astype(k_ref.dtype)
    vT_ref[0] = _dot_nt(wvT_ref[...], mn).astype(vT_ref.dtype)


def _kv_mem(mem, g_row, wk, wvT):
    nb, m, d = mem.shape
    return pl.pallas_call(
        _kv_mem_kernel,
        out_shape=(jax.ShapeDtypeStruct((nb, m, d), MXU_DTYPE), jax.ShapeDtypeStruct((nb, d, m), MXU_DTYPE)),
        grid=(nb,),
        in_specs=[pl.BlockSpec((1, m, d), lambda b: (b, 0, 0)), _const_spec((1, d)),
                  _const_spec((d, d)), _const_spec((d, d))],
        out_specs=(pl.BlockSpec((1, m, d), lambda b: (b, 0, 0)), pl.BlockSpec((1, d, m), lambda b: (b, 0, 0))),
        compiler_params=_params(("parallel",), "kv_mem"),
        name="kv_mem",
    )(mem, g_row, wk, wvT)


def _head_norm_rope(t, g_col, cos, sin):
    ms = jnp.mean(t * t, axis=0, keepdims=True)
    t = t * lax.rsqrt(ms + EPS) * g_col
    q4 = HEAD_DIM // 4
    sw = jnp.concatenate([t[q4:2 * q4], t[0:q4], t[3 * q4:], t[2 * q4:3 * q4]], axis=0)
    return t * cos + sw * sin


def _inproj_kernel(x_ref, gmix_ref, w_ref, gq_ref, gk_ref, cos_ref, sin_ref, gsg_ref, wsT_ref, bs_ref, wsgo_ref,
                   qT_ref, k_ref, vT_ref, gattn_ref, sgp_ref):
    tm = x_ref.shape[1]
    h = _rmsnorm_rows(x_ref[0], gmix_ref[...]).astype(MXU_DTYPE)

    def proj_t(lo, hi):
        return _dot_nt(w_ref[lo:hi, :], h)

    cos = cos_ref[...]
    sin = sin_ref[...]

    q_t = proj_t(_Q0, _K0)
    scale = math.log2(math.e) / math.sqrt(HEAD_DIM)
    for hd in range(N_HEADS):
        r = _head_norm_rope(q_t[hd * HEAD_DIM:(hd + 1) * HEAD_DIM], gq_ref[...], cos, sin) * scale
        qT_ref[0, hd * HEAD_DIM:(hd + 1) * HEAD_DIM, :] = r.astype(qT_ref.dtype)

    k_t = proj_t(_K0, _V0)
    k_rot = jnp.concatenate(
        [_head_norm_rope(k_t[g * HEAD_DIM:(g + 1) * HEAD_DIM], gk_ref[...], cos, sin) for g in range(N_KV_HEADS)],
        axis=0)
    k_tok = k_rot.T
    for g in range(N_KV_HEADS):
        k_ref[0, g] = k_tok[:, g * HEAD_DIM:(g + 1) * HEAD_DIM].astype(k_ref.dtype)

    vT_ref[0] = proj_t(_V0, _Z0).astype(vT_ref.dtype)

    z_t = _gelu(proj_t(_Z0, _G0))
    u = z_t[:SG_WIDTH]
    vn = _rmsnorm_cols(z_t[SG_WIDTH:], gsg_ref[...]).astype(MXU_DTYPE)
    n_chunks = tm // SG_CHUNK
    sv_groups = []
    for g in range(SG_GROUPS):
        vg = vn[g * SG_GROUP_DIM:(g + 1) * SG_GROUP_DIM]
        lhs = jnp.concatenate([vg[:, c * SG_CHUNK:(c + 1) * SG_CHUNK] for c in range(n_chunks)], axis=0)
        r = _dot(lhs, wsT_ref[g]) + bs_ref[g]
        sv_groups.append(jnp.concatenate(
            [r[c * SG_GROUP_DIM:(c + 1) * SG_GROUP_DIM] for c in range(n_chunks)], axis=1))
    sg = (u * jnp.concatenate(sv_groups, axis=0)).astype(MXU_DTYPE)
    sg_branch = _dot(wsgo_ref[...], sg)

    gates = _sigmoid(proj_t(_G0, _IN_WIDTH))
    gattn_ref[0] = gates[:D_MODEL].astype(gattn_ref.dtype)
    sgp_ref[0] = (gates[D_MODEL:] * sg_branch).astype(sgp_ref.dtype)


def _inproj(x, gmix, w_inT, gq, gk, cos_t, sin_t, gsg, wsT, bs, wsgoT, tm):
    b, s, d = x.shape
    grid = (b, s // tm)
    tok = lambda rows: pl.BlockSpec((1, rows, tm), lambda i, j: (i, 0, j))
    return pl.pallas_call(
        _inproj_kernel,
        out_shape=(jax.ShapeDtypeStruct((b, ATTN_Q_WIDTH, s), MXU_DTYPE),
                   jax.ShapeDtypeStruct((b, N_KV_HEADS, s, HEAD_DIM), MXU_DTYPE),
                   jax.ShapeDtypeStruct((b, ATTN_KV_WIDTH, s), MXU_DTYPE),
                   jax.ShapeDtypeStruct((b, D_MODEL, s), MXU_DTYPE),
                   jax.ShapeDtypeStruct((b, D_MODEL, s), jnp.float32)),
        grid=grid,
        in_specs=[pl.BlockSpec((1, tm, d), lambda i, j: (i, j, 0)),
                  _const_spec((1, d)), _const_spec(w_inT.shape),
                  _const_spec((HEAD_DIM, 1)), _const_spec((HEAD_DIM, 1)),
                  pl.BlockSpec((HEAD_DIM, tm), lambda i, j: (0, j)),
                  pl.BlockSpec((HEAD_DIM, tm), lambda i, j: (0, j)),
                  _const_spec((SG_WIDTH, 1)), _const_spec(wsT.shape), _const_spec(bs.shape),
                  _const_spec(wsgoT.shape)],
        out_specs=(tok(ATTN_Q_WIDTH),
                   pl.BlockSpec((1, N_KV_HEADS, tm, HEAD_DIM), lambda i, j: (i, 0, j, 0)),
                   tok(ATTN_KV_WIDTH), tok(D_MODEL), tok(D_MODEL)),
        compiler_params=_params(("parallel", "parallel"), "inproj"),
        name="inproj",
    )(x, gmix, w_inT, gq, gk, cos_t, sin_t, gsg, wsT, bs, wsgoT)


def _attn_kernel(qT_ref, k_ref, vT_ref, o_ref):
    tq = qT_ref.shape[2]
    k = k_ref[0, 0]
    vT = vT_ref[0]
    vT1 = jnp.concatenate([vT, jnp.ones((GATE_ROWS, vT.shape[1]), vT.dtype)], axis=0)
    for pair in range(KV_GROUP // 2):
        heads = (2 * pair, 2 * pair + 1)
        qT2 = jnp.concatenate([qT_ref[0, a * HEAD_DIM:(a + 1) * HEAD_DIM, :] for a in heads], axis=1)
        sT = _dot(k, qT2).astype(MXU_DTYPE)
        m = jnp.max(sT, axis=0, keepdims=True)
        p = jnp.exp2(sT - m)
        o = _dot(vT1, p)
        oT = o[:HEAD_DIM] / o[HEAD_DIM:HEAD_DIM + 1]
        for i, a in enumerate(heads):
            o_ref[0, a * HEAD_DIM:(a + 1) * HEAD_DIM, :] = oT[:, i * tq:(i + 1) * tq].astype(o_ref.dtype)


def _attention(qT, k, vT, tq):
    b, _, s = qT.shape
    rows = KV_GROUP * HEAD_DIM
    return pl.pallas_call(
        _attn_kernel,
        out_shape=jax.ShapeDtypeStruct((b, ATTN_Q_WIDTH, s), MXU_DTYPE),
        grid=(b, N_KV_HEADS, s // tq),
        in_specs=[pl.BlockSpec((1, rows, tq), lambda i, g, j: (i, g, j)),
                  pl.BlockSpec((1, 1, s, HEAD_DIM), lambda i, g, j: (i, g, 0, 0)),
                  pl.BlockSpec((1, HEAD_DIM, s), lambda i, g, j: (i, g, 0))],
        out_specs=pl.BlockSpec((1, rows, tq), lambda i, g, j: (i, g, j)),
        compiler_params=_params(("parallel", "parallel", "parallel"), "attn"),
        name="attn",
    )(qT, k, vT)


def _merge_kernel(x_ref, attn_ref, gattn_ref, sgp_ref, wao_ref, wout_ref, gxa_ref, wq_ref, km_ref, vmT_ref,
                  wo_ref, gffn_ref, x2_ref, hp_ref):
    xT = x_ref[0].T
    attn_branch = _dot(wao_ref[...], attn_ref[0])
    mix = gattn_ref[0].astype(jnp.float32) * attn_branch + sgp_ref[0]
    x1 = xT + _dot(wout_ref[...], mix.astype(MXU_DTYPE))

    hx = _rmsnorm_cols(x1, gxa_ref[...]).astype(MXU_DTYPE)
    qx = (_dot(wq_ref[...], hx) * (1.0 / math.sqrt(XA_HEAD_DIM))).astype(MXU_DTYPE)
    outs = []
    for hd in range(XA_HEADS):
        lo, hi = hd * XA_HEAD_DIM, (hd + 1) * XA_HEAD_DIM
        sT = _dot(km_ref[0, :, lo:hi], qx[lo:hi])
        m = jnp.max(sT, axis=0, keepdims=True)
        p = jnp.exp(sT - m)
        l = jnp.sum(p, axis=0, keepdims=True)
        outs.append((_dot(vmT_ref[0, lo:hi, :], p.astype(MXU_DTYPE)) / l).astype(MXU_DTYPE))
    x2 = x1 + _dot(wo_ref[...], jnp.concatenate(outs, axis=0))
    x2_ref[0] = x2
    hp_ref[0] = _rmsnorm_cols(x2, gffn_ref[...]).astype(hp_ref.dtype)


def _merge(x, attnT, gattnT, sgpT, waoT, woutT, gxa, wqT, kmem, vmemT, woT, gffn, tm):
    b, s, d = x.shape
    m = kmem.shape[1]
    tok = pl.BlockSpec((1, d, tm), lambda i, j: (i, 0, j))
    wspec = _const_spec((d, d))
    col = _const_spec((d, 1))
    return pl.pallas_call(
        _merge_kernel,
        out_shape=(jax.ShapeDtypeStruct((b, d, s), jnp.float32), jax.ShapeDtypeStruct((b, d, s), MXU_DTYPE)),
        grid=(b, s // tm),
        in_specs=[pl.BlockSpec((1, tm, d), lambda i, j: (i, j, 0)), tok, tok, tok, wspec, wspec, col, wspec,
                  pl.BlockSpec((1, m, d), lambda i, j: (i, 0, 0)), pl.BlockSpec((1, d, m), lambda i, j: (i, 0, 0)),
                  wspec, col],
        out_specs=(tok, tok),
        compiler_params=_params(("parallel", "parallel"), "merge"),
        name="merge",
    )(x, attnT, gattnT, sgpT, waoT, woutT, gxa, wqT, kmem, vmemT, woT, gffn)


def _sort16_network():
    def merge(lo, hi, r):
        step = 2 * r
        if step < hi - lo:
            yield from merge(lo, hi, step)
            yield from merge(lo + r, hi, step)
            yield from ((i, i + r) for i in range(lo + r, hi - r, step))
        else:
            yield (lo, lo + r)

    def sort(lo, hi):
        if hi > lo:
            mid = lo + (hi - lo) // 2
            yield from sort(lo, mid)
            yield from sort(mid + 1, hi)
            yield from merge(lo, hi, 1)

    return tuple(sort(0, PEER_TOPK - 1))


def _merge16_network():
    out, k = [], PEER_TOPK // 2
    while k >= 1:
        out += [(i, i + k) for i in range(PEER_TOPK) if i % (2 * k) < k]
        k //= 2
    return tuple(out)


_SORT16 = _sort16_network()
_MERGE16 = _merge16_network()


def _exchange(x, net):
    for i, j in net:
        x[i], x[j] = jnp.maximum(x[i], x[j]), jnp.minimum(x[i], x[j])
    return x


def _top_values(s, top_ref):
    assert s.shape[0] == PEER_TOPK * SUBLANES
    x = _exchange([s[v * SUBLANES:(v + 1) * SUBLANES, :] for v in range(PEER_TOPK)], _SORT16)
    shift = SUBLANES // 2
    while shift >= 1:
        x = _exchange([jnp.maximum(x[v], pltpu.roll(x[PEER_TOPK - 1 - v], shift, 0)) for v in range(PEER_TOPK)],
                      _MERGE16)
        shift //= 2
    for r in range(PEER_TOPK):
        top_ref[r:r + 1, :] = x[r][0:1, :]


def _stair_candidates(v1, v2):
    half = PEER_TOPK // 2
    row8 = lax.broadcasted_iota(jnp.int32, (half, 1), 0)
    cands = [v1 + v2[0:1], v1[0:1] + v2[half:]]
    for bb in range(1, half):
        a_max = PEER_TOPK // (bb + 1) - 1
        c = v1[:half] + v2[bb:bb + 1]
        cands.append(jnp.where(row8 <= a_max, c, NEG_INF))
    return jnp.concatenate(cands, axis=0)


def _peer_sel_kernel(hp_ref, wq_ref, k1_ref, k2_ref, nb_ref, p1_ref, r2_ref, p2_ref, v1_s, v2_s, c_s):
    qp = _dot(wq_ref[...], hp_ref[0]).astype(MXU_DTYPE)
    for hd in range(PEER_HEADS):
        base = hd * PEER_D_KEY
        s1 = _dot(k1_ref[hd], qp[base:base + PEER_HALF])
        s2 = _dot(k2_ref[hd], qp[base + PEER_HALF:base + PEER_D_KEY])
        _top_values(s1, v1_s)
        _top_values(s2, v2_s)
        v1 = v1_s[...]
        v2 = v2_s[...]
        cand = _stair_candidates(v1, v2)
        pad = jnp.full((PEER_N_KEYS - cand.shape[0], cand.shape[1]), NEG_INF, jnp.float32)
        _top_values(jnp.concatenate([cand, pad], axis=0), c_s)
        t = c_s[PEER_TOPK - 1:PEER_TOPK, :]
        m1 = v1[0:1]
        m2 = v2[0:1]
        z = jnp.sum(jnp.where(cand >= t, jnp.exp(cand - (m1 + m2)), 0.0), axis=0, keepdims=True)
        nb_top = jnp.zeros_like(v1)
        for b in range(PEER_TOPK):
            nb_top = nb_top + jnp.where(v1 + v2[b:b + 1] >= t, 1.0, 0.0)
        nb = jnp.zeros_like(s1)
        r2 = jnp.full(s2.shape, float(PEER_TOPK), jnp.float32)
        for a in reversed(range(PEER_TOPK)):
            nb = jnp.where(s1 == v1[a:a + 1], nb_top[a:a + 1], nb)
            r2 = jnp.where(s2 == v2[a:a + 1], float(a), r2)
        nb_ref[0, hd] = nb
        p1_ref[0, hd] = jnp.exp(s1 - m1) / z
        r2_ref[0, hd] = r2.astype(r2_ref.dtype)
        p2_ref[0, hd] = jnp.exp(s2 - m2).astype(p2_ref.dtype)


def _peer_sel(hpT, wpqT, k1, k2, tm):
    b, d, s = hpT.shape
    shape = (b, PEER_HEADS, PEER_N_KEYS, s)
    words = jax.ShapeDtypeStruct(shape, jnp.float32)
    sel = jax.ShapeDtypeStruct(shape, GATE_DTYPE)
    sel_spec = pl.BlockSpec((1, PEER_HEADS, PEER_N_KEYS, tm), lambda i, j: (i, 0, 0, j))
    return pl.pallas_call(
        _peer_sel_kernel,
        out_shape=(words, words, sel, sel),
        grid=(b, s // tm),
        in_specs=[pl.BlockSpec((1, d, tm), lambda i, j: (i, 0, j)), _const_spec(wpqT.shape),
                  _const_spec(k1.shape), _const_spec(k2.shape)],
        out_specs=(sel_spec, sel_spec, sel_spec, sel_spec),
        scratch_shapes=[pltpu.VMEM((PEER_TOPK, tm), jnp.float32) for _ in range(3)],
        compiler_params=_params(("parallel", "parallel"), "peer_sel"),
        name="peer_sel",
    )(hpT, wpqT, k1, k2)


def _peer_mix_kernel(hp_ref, ed_ref, euT_ref, nb_ref, p1_ref, r2_ref, p2_ref, x2_ref, gfin_ref, y_ref,
                     acc_ref, act_ref, w_ref, row_s, key_s):
    c = pl.program_id(2)
    n_c = pl.num_programs(2)
    tt = hp_ref.shape[2]
    n_i = ed_ref.shape[0] // PEER_N_KEYS
    jb = GATE_ROWS
    n_k = PEER_N_KEYS // jb

    @pl.when(c == 0)
    def _():
        acc_ref[...] = jnp.zeros_like(acc_ref)
        key_s[0] = r2_ref[0]
        key_s[1] = p2_ref[0]

    row_s[0] = nb_ref[0]
    row_s[1] = p1_ref[0]

    act_ref[...] = _gelu(_dot(ed_ref[...], hp_ref[0])).astype(act_ref.dtype)

    def row_bcast(row):
        return jnp.broadcast_to(row, (jb, LANES)).astype(GATE_DTYPE)

    zero = jnp.zeros((jb, LANES), GATE_DTYPE)

    for lb in range(tt // LANES):
        ls = pl.ds(lb * LANES, LANES)
        for il in range(n_i):
            accs = [None] * n_k
            for hd in range(PEER_HEADS):
                bn = row_bcast(row_s[0, hd, il:il + 1, ls])
                bp = row_bcast(row_s[1, hd, il:il + 1, ls])
                for k in range(n_k):
                    r2v = key_s[0, hd, k * jb:(k + 1) * jb, ls]
                    p2v = key_s[1, hd, k * jb:(k + 1) * jb, ls]
                    term = bp * jnp.where(r2v < bn, p2v, zero)
                    accs[k] = term if accs[k] is None else accs[k] + term
            for k in range(n_k):
                rows = pl.ds(il * PEER_N_KEYS + k * jb, jb)
                w_ref[rows, ls] = (accs[k] * act_ref[rows, ls]).astype(w_ref.dtype)

    acc_ref[...] += _dot(euT_ref[0], w_ref[...])

    @pl.when(c == n_c - 1)
    def _():
        x3 = x2_ref[0] + acc_ref[...]
        y_ref[0] = _rmsnorm_cols(x3, gfin_ref[...]).T.astype(y_ref.dtype)


def _peer_mix(hpT, e_down, e_upT, nb, p1, r2, p2, x2T, gfin, tt):
    b, d, s = hpT.shape
    n_e = e_down.shape[0]
    ec = PEER_CHUNK
    n_i = ec // PEER_N_KEYS
    tok = lambda rows: pl.BlockSpec((1, rows, tt), lambda i, j, c: (i, 0, j))
    row_spec = pl.BlockSpec((1, PEER_HEADS, n_i, tt), lambda i, j, c: (i, 0, c, j))
    full_spec = pl.BlockSpec((1, PEER_HEADS, PEER_N_KEYS, tt), lambda i, j, c: (i, 0, 0, j))
    return pl.pallas_call(
        _peer_mix_kernel,
        out_shape=jax.ShapeDtypeStruct((b, s, d), jnp.float32),
        grid=(b, s // tt, n_e // ec),
        in_specs=[tok(d),
                  pl.BlockSpec((ec, d), lambda i, j, c: (c, 0)),
                  pl.BlockSpec((1, d, ec), lambda i, j, c: (c, 0, 0)),
                  row_spec, row_spec, full_spec, full_spec, tok(d), _const_spec((d, 1))],
        out_specs=pl.BlockSpec((1, tt, d), lambda i, j, c: (i, j, 0)),
        scratch_shapes=[pltpu.VMEM((d, tt), jnp.float32), pltpu.VMEM((ec, tt), GATE_DTYPE),
                        pltpu.VMEM((ec, tt), MXU_DTYPE),
                        pltpu.VMEM((2, PEER_HEADS, n_i, tt), jnp.float32),
                        pltpu.VMEM((2, PEER_HEADS, PEER_N_KEYS, tt), GATE_DTYPE)],
        compiler_params=_params(("parallel", "parallel", "arbitrary"), "peer_mix"),
        name="peer_mix",
    )(hpT, e_down, e_upT, nb, p1, r2, p2, x2T, gfin)


def _rope_tables(s):
    rows = s // GRID_W
    row_pos = jnp.repeat(jnp.arange(rows, dtype=jnp.float32), GRID_W)
    col_pos = jnp.tile(jnp.arange(GRID_W, dtype=jnp.float32), rows)
    n_freq = HEAD_DIM // 4
    inv_freq = ROPE_THETA ** (-jnp.arange(n_freq, dtype=jnp.float32) / n_freq)
    ang_r = inv_freq[:, None] * row_pos[None, :]
    ang_c = inv_freq[:, None] * col_pos[None, :]
    cos_t = jnp.concatenate([jnp.cos(ang_r), jnp.cos(ang_r), jnp.cos(ang_c), jnp.cos(ang_c)], axis=0)
    sin_t = jnp.concatenate([-jnp.sin(ang_r), jnp.sin(ang_r), -jnp.sin(ang_c), jnp.sin(ang_c)], axis=0)
    return cos_t, sin_t


def _tiles(s):
    attn = max(SG_CHUNK, ATTN_SCORE_BYTES // (2 * 2 * s))
    return dict(inproj=min(512, s), attn=min(attn, s), merge=min(512, s), sel=min(256, s), mix=min(512, s))


def _layer(x, kmem, vmemT, w):
    s = x.shape[1]
    t = _tiles(s)
    cos_t, sin_t = _rope_tables(s)
    qT, k, vT, gattnT, sgpT = _inproj(x, w["gmix"], w["w_inT"], w["gq"], w["gk"], cos_t, sin_t, w["gsg"],
                                      w["wsT"], w["bs"], w["wsgoT"], t["inproj"])
    attnT = _attention(qT, k, vT, t["attn"])
    x2T, hpT = _merge(x, attnT, gattnT, sgpT, w["waoT"], w["woutT"], w["gxa"], w["wqxT"], kmem, vmemT,
                      w["woxT"], w["gffn"], t["merge"])
    nb, p1, r2, p2 = _peer_sel(hpT, w["wpqT"], w["k1"], w["k2"], t["sel"])
    return _peer_mix(hpT, w["e_down"], w["e_upT"], nb, p1, r2, p2, x2T, w["gfin"], t["mix"])


def kernel(x_prompt, x_sample, mem_prompt, mem_sample, norm_mix_g, w_in, q_norm_g, k_norm_g, sg_norm_g, sg_w,
           sg_b, w_attn_o, w_sg_o, w_out, norm_xa_g, norm_mem_g, wq_xa, wkv_xa, wo_xa, norm_ffn_g, w_peer_q,
           peer_k1, peer_k2, expert_down, expert_up, final_norm_g):
    assert w_in.shape[0] == 1, "single-layer trunk"
    w = _prep_weights(norm_mix_g[0], w_in[0], q_norm_g[0], k_norm_g[0], sg_norm_g[0], sg_w[0], sg_b[0],
                      w_attn_o[0], w_sg_o[0], w_out[0], norm_xa_g[0], norm_mem_g[0], wq_xa[0], wkv_xa[0],
                      wo_xa[0], norm_ffn_g[0], w_peer_q[0], peer_k1[0], peer_k2[0], expert_down[0],
                      expert_up[0], final_norm_g)
    outs = []
    for x, mem in ((x_prompt, mem_prompt), (x_sample, mem_sample)):
        kmem, vmemT = _kv_mem(mem, w["gmem"], w["wk"], w["wvT"])
        outs.append(_layer(x, kmem, vmemT, w))
    return tuple(outs)


def _prep_weights(norm_mix_g, w_in, q_norm_g, k_norm_g, sg_norm_g, sg_w, sg_b, w_attn_o, w_sg_o, w_out,
                  norm_xa_g, norm_mem_g, wq_xa, wkv_xa, wo_xa, norm_ffn_g, w_peer_q, peer_k1, peer_k2,
                  expert_down, expert_up, final_norm_g):
    cast_t = lambda a: a.astype(MXU_DTYPE).T
    col = lambda g: g.reshape(-1, 1).astype(jnp.float32)
    n_e, d = expert_up.shape
    e_upT = jnp.swapaxes(expert_up.astype(MXU_DTYPE).reshape(n_e // PEER_CHUNK, PEER_CHUNK, d), 1, 2)
    return dict(
        gmix=norm_mix_g.reshape(1, -1), w_inT=cast_t(w_in), gq=col(q_norm_g), gk=col(k_norm_g),
        gsg=col(sg_norm_g), wsT=jnp.swapaxes(sg_w, 1, 2).astype(MXU_DTYPE),
        bs=sg_b.reshape(SG_GROUPS, 1, SG_CHUNK), wsgoT=cast_t(w_sg_o),
        waoT=cast_t(w_attn_o), woutT=cast_t(w_out), gxa=col(norm_xa_g), wqxT=cast_t(wq_xa),
        woxT=cast_t(wo_xa), gffn=col(norm_ffn_g), wpqT=cast_t(w_peer_q),
        k1=peer_k1.astype(MXU_DTYPE), k2=peer_k2.astype(MXU_DTYPE),
        e_down=expert_down.astype(MXU_DTYPE), e_upT=e_upT, gfin=col(final_norm_g),
        gmem=norm_mem_g.reshape(1, -1), wk=wkv_xa[:, :D_MODEL].astype(MXU_DTYPE), wvT=cast_t(wkv_xa[:, D_MODEL:]),
    )
```
